```python
import jax, jax.numpy as jnp
from jax import lax
import numpy as np

D_MODEL = 1024
BATCH = 32
SEQ = 2048
DEPTH = 1

EPS = 1e-6
LN_EPS = 1e-5
N_Q_HEADS = 8
N_KV_HEADS = 2
HEAD_DIM = 64
Q_PER_KV = N_Q_HEADS // N_KV_HEADS
ATTN_WIDTH = N_Q_HEADS * HEAD_DIM
KV_WIDTH = N_KV_HEADS * HEAD_DIM
WINDOW = 128
BLOCK = 128
GMLP_GROUPS = 4
GMLP_GROUP_WIDTH = 128
GMLP_WIDTH = GMLP_GROUPS * GMLP_GROUP_WIDTH
CHUNK = 128
D_FF = -(-8 * D_MODEL // (3 * 256)) * 256
IN_WIDTH = ATTN_WIDTH + 2 * KV_WIDTH + 2 * GMLP_WIDTH + 2 * D_MODEL

kernel_name = "hybrid_swa_sink_gmlp_gated_block"


def rms_norm(x, g):
    xf = x.astype(jnp.float32)
    y = xf * lax.rsqrt(jnp.mean(xf * xf, axis=-1, keepdims=True) + EPS)
    return (y * g.astype(jnp.float32)).astype(x.dtype)


def layer_norm(x, g, b):
    xf = x.astype(jnp.float32)
    mu = jnp.mean(xf, axis=-1, keepdims=True)
    var = jnp.mean(jnp.square(xf - mu), axis=-1, keepdims=True)
    y = (xf - mu) * lax.rsqrt(var + LN_EPS)
    return (y * g.astype(jnp.float32) + b.astype(jnp.float32)).astype(x.dtype)


def alibi_slopes(n_heads):
    return 2.0 ** (-8.0 * jnp.arange(1, n_heads + 1, dtype=jnp.float32) / n_heads)


def banded_sink_attention(q, k, v, sinks):
    B, S = q.shape[0], q.shape[1]
    nb = S // BLOCK
    qb = q.reshape(B, nb, BLOCK, N_KV_HEADS, Q_PER_KV, HEAD_DIM)
    pad = ((0, 0), (BLOCK, 0), (0, 0), (0, 0))
    kp = jnp.pad(k, pad).reshape(B, nb + 1, BLOCK, N_KV_HEADS, HEAD_DIM)
    vp = jnp.pad(v, pad).reshape(B, nb + 1, BLOCK, N_KV_HEADS, HEAD_DIM)
    kb = jnp.concatenate([kp[:, :-1], kp[:, 1:]], axis=2)
    vb = jnp.concatenate([vp[:, :-1], vp[:, 1:]], axis=2)
    scale = HEAD_DIM ** -0.5
    s = jnp.einsum('bnqhgd,bnkhd->bhgnqk', qb, kb).astype(jnp.float32) * scale
    a = jnp.arange(BLOCK)[:, None]
    j = jnp.arange(2 * BLOCK)[None, :]
    rel = BLOCK + a - j
    blk = jnp.arange(nb)[:, None, None]
    s_abs = (blk - 1) * BLOCK + j[None]
    valid = (rel[None] >= 0) & (rel[None] < WINDOW) & (s_abs >= 0)
    slopes = alibi_slopes(N_Q_HEADS).reshape(N_KV_HEADS, Q_PER_KV)
    alibi = -slopes[:, :, None, None, None] * rel.astype(jnp.float32)[None, None, None]
    logits = jnp.where(valid, s + alibi, -1e30)
    sink = sinks.astype(jnp.float32).reshape(N_KV_HEADS, Q_PER_KV)[:, :, None, None]
    m = jnp.maximum(jnp.max(logits, axis=-1), sink)
    p = jnp.exp(logits - m[..., None])
    denom = jnp.sum(p, axis=-1) + jnp.exp(sink - m)
    probs = (p / denom[..., None]).astype(v.dtype)
    o = jnp.einsum('bhgnqk,bnkhd->bnqhgd', probs, vb)
    return o.reshape(B, S, ATTN_WIDTH)


def chunked_spatial_gating(z, ln_g, ln_b, w_s, b_s):
    B, S = z.shape[0], z.shape[1]
    u, v = jnp.split(z, 2, axis=-1)
    v = layer_norm(v, ln_g, ln_b)
    nc = S // CHUNK
    vc = v.reshape(B, nc, CHUNK, GMLP_GROUPS, GMLP_GROUP_WIDTH)
    causal = jnp.tril(jnp.ones((CHUNK, CHUNK), dtype=w_s.dtype))
    w = w_s * causal[None]
    f = jnp.einsum('gts,bnsgc->bntgc', w, vc) + b_s.T[:, :, None]
    return u * f.reshape(B, S, GMLP_WIDTH)


def mixer_block(xn, w_in, attn_sinks, gmlp_ln_g, gmlp_ln_b, gmlp_w_s, gmlp_b_s,
                w_attn_branch, w_gmlp_branch, w_out):
    B, S = xn.shape[0], xn.shape[1]
    proj = jnp.einsum('bsd,de->bse', xn, w_in)
    splits = np.cumsum([ATTN_WIDTH, KV_WIDTH, KV_WIDTH, 2 * GMLP_WIDTH, D_MODEL])
    q, k, v, zg, g_a, g_b = jnp.split(proj, splits, axis=-1)
    q = q.reshape(B, S, N_Q_HEADS, HEAD_DIM)
    k = k.reshape(B, S, N_KV_HEADS, HEAD_DIM)
    v = v.reshape(B, S, N_KV_HEADS, HEAD_DIM)
    attn = banded_sink_attention(q, k, v, attn_sinks)
    gm = chunked_spatial_gating(jax.nn.gelu(zg, approximate=False),
                                gmlp_ln_g, gmlp_ln_b, gmlp_w_s, gmlp_b_s)
    br_a = jnp.einsum('bse,ed->bsd', attn, w_attn_branch)
    br_b = jnp.einsum('bse,ed->bsd', gm, w_gmlp_branch)
    merged = jax.nn.sigmoid(g_a) * br_a + jax.nn.sigmoid(g_b) * br_b
    return jnp.einsum('bsd,de->bse', merged, w_out)


def swiglu(x, w_gate, w_up, w_down):
    h = jax.nn.silu(jnp.einsum('bsd,df->bsf', x, w_gate)) * jnp.einsum('bsd,df->bsf', x, w_up)
    return jnp.einsum('bsf,fd->bsd', h, w_down)


def setup_inputs(seed: int = 0) -> dict:
    key = jax.random.key(seed)
    ks = jax.random.split(key, 20)
    f32 = jnp.float32

    def nrm(k, shape, scale):
        return jax.random.normal(k, shape, f32) * scale

    def gain(k, n):
        return 1.0 + 0.05 * jax.random.normal(k, (DEPTH, n), f32)

    return {
        "x": jax.random.normal(ks[0], (BATCH, SEQ, D_MODEL), f32),
        "norm_mix_pre": gain(ks[1], D_MODEL),
        "w_in": nrm(ks[2], (DEPTH, D_MODEL, IN_WIDTH), D_MODEL ** -0.5),
        "attn_sinks": nrm(ks[3], (DEPTH, N_Q_HEADS), 0.5),
        "gmlp_ln_g": gain(ks[4], GMLP_WIDTH),
        "gmlp_ln_b": nrm(ks[5], (DEPTH, GMLP_WIDTH), 0.02),
        "gmlp_w_s": nrm(ks[6], (DEPTH, GMLP_GROUPS, CHUNK, CHUNK), CHUNK ** -0.5),
        "gmlp_b_s": 1.0 + 0.1 * jax.random.normal(ks[7], (DEPTH, GMLP_GROUPS, CHUNK), f32),
        "w_attn_branch": nrm(ks[8], (DEPTH, ATTN_WIDTH, D_MODEL), ATTN_WIDTH ** -0.5),
        "w_gmlp_branch": nrm(ks[9], (DEPTH, GMLP_WIDTH, D_MODEL), GMLP_WIDTH ** -0.5),
        "w_out": nrm(ks[10], (DEPTH, D_MODEL, D_MODEL), D_MODEL ** -0.5),
        "norm_mix_post": gain(ks[11], D_MODEL),
        "norm_ffn_pre": gain(ks[12], D_MODEL),
        "w_ffn_gate": nrm(ks[13], (DEPTH, D_MODEL, D_FF), D_MODEL ** -0.5),
        "w_ffn_up": nrm(ks[14], (DEPTH, D_MODEL, D_FF), D_MODEL ** -0.5),
        "w_ffn_down": nrm(ks[15], (DEPTH, D_FF, D_MODEL), D_FF ** -0.5),
        "norm_ffn_post": gain(ks[16], D_MODEL),
    }


def reference(x, norm_mix_pre, w_in, attn_sinks, gmlp_ln_g, gmlp_ln_b, gmlp_w_s, gmlp_b_s,
              w_attn_branch, w_gmlp_branch, w_out, norm_mix_post, norm_ffn_pre,
              w_ffn_gate, w_ffn_up, w_ffn_down, norm_ffn_post):
    h = x
    for l in range(DEPTH):
        xn = rms_norm(h, norm_mix_pre[l])
        mix = mixer_block(xn, w_in[l], attn_sinks[l], gmlp_ln_g[l], gmlp_ln_b[l],
                          gmlp_w_s[l], gmlp_b_s[l], w_attn_branch[l], w_gmlp_branch[l], w_out[l])
        h = h + rms_norm(mix, norm_mix_post[l])
        hn = rms_norm(h, norm_ffn_pre[l])
        ff = swiglu(hn, w_ffn_gate[l], w_ffn_up[l], w_ffn_down[l])
        h = h + rms_norm(ff, norm_ffn_post[l])
    return h
```

```python
import functools

import jax
import jax.numpy as jnp
import numpy as np
from jax import lax
from jax.experimental import pallas as pl
from jax.experimental.pallas import tpu as pltpu

D_MODEL = 1024
EPS = 1e-6
LN_EPS = 1e-5
N_Q_HEADS = 8
N_KV_HEADS = 2
HEAD_DIM = 64
Q_PER_KV = N_Q_HEADS // N_KV_HEADS
ATTN_WIDTH = N_Q_HEADS * HEAD_DIM
KV_WIDTH = N_KV_HEADS * HEAD_DIM
BLOCK = 128
GMLP_GROUPS = 4
GMLP_WIDTH = 512
D_FF = 2816
Q_OFF = 0
K_OFF = ATTN_WIDTH
V_OFF = K_OFF + KV_WIDTH
Z_OFF = V_OFF + KV_WIDTH
GA_OFF = Z_OFF + 2 * GMLP_WIDTH
GB_OFF = GA_OFF + D_MODEL
IN_WIDTH = GB_OFF + D_MODEL

V7X_LANES = 128
V7X_VMEM_BYTES = 64 * 1024 * 1024
MIXER_TILE = 256
FFN_TILE = 256

_BF16 = jnp.bfloat16
_F32 = jnp.float32
_SQRT_HALF = np.float32(np.sqrt(0.5))
_MASKED = -1e30


def _dot(a, b):
    return jnp.dot(a, b, preferred_element_type=_F32)


def _rms_norm(x, gain):
    y = x * lax.rsqrt(jnp.mean(x * x, axis=-1, keepdims=True) + EPS)
    return y * gain


def _gelu(x):
    return 0.5 * x * (1.0 + lax.erf(x * _SQRT_HALF))


def _sigmoid(x):
    return 1.0 / (1.0 + jnp.exp(-x))


def _mixer_kernel(x_ref, gpre_ref, win_ref, sinks_ref, lng_ref, lnb_ref, ws_ref,
                  bst_ref, wa_ref, wb_ref, wo_ref, gpost_ref, h_ref, kvt_ref, *,
                  tile):
    nb = tile // BLOCK
    j = pl.program_id(1)

    @pl.when(j == 0)
    def _():
        kvt_ref[...] = jnp.zeros_like(kvt_ref)

    x = x_ref[...]
    xn = _rms_norm(x, gpre_ref[...]).astype(_BF16)

    qkv = _dot(xn, win_ref[:, Q_OFF:Z_OFF])
    q = (qkv[:, Q_OFF:K_OFF] * (HEAD_DIM ** -0.5)).astype(_BF16)
    lane = lax.broadcasted_iota(jnp.int32, (tile, V7X_LANES), 1)
    lo = lane < HEAD_DIM
    kf = qkv[:, K_OFF:V_OFF]
    vf = qkv[:, V_OFF:Z_OFF]
    kr = pltpu.roll(kf, HEAD_DIM, 1)
    vr = pltpu.roll(vf, HEAD_DIM, 1)
    kvt = jnp.concatenate(
        [jnp.where(lo, kf, kr), jnp.where(lo, kr, kf),
         jnp.where(lo, vf, vr), jnp.where(lo, vr, vf)], axis=1).astype(_BF16)
    kvt_prev = kvt_ref[...]
    kvt_ref[...] = kvt[tile - BLOCK:, :]

    qi = lax.broadcasted_iota(jnp.int32, (BLOCK, 2 * BLOCK), 0)
    ki = lax.broadcasted_iota(jnp.int32, (BLOCK, 2 * BLOCK), 1)
    rel = BLOCK + qi - ki
    band = (rel >= 0) & (rel < BLOCK)
    relf = rel.astype(_F32)
    lo_b = lax.broadcasted_iota(jnp.int32, (BLOCK, V7X_LANES), 1) < HEAD_DIM
    lo_2b = lax.broadcasted_iota(jnp.int32, (2 * BLOCK, V7X_LANES), 1) < HEAD_DIM
    zeros_kv = jnp.zeros((2 * BLOCK, V7X_LANES), _BF16)

    attn_rows = []
    for i in range(nb):
        rows = slice(i * BLOCK, (i + 1) * BLOCK)
        prev = kvt_prev if i == 0 else kvt[(i - 1) * BLOCK:i * BLOCK, :]
        kvt2 = jnp.concatenate([prev, kvt[rows, :]], axis=0)
        if i == 0:
            valid = band & ((ki >= BLOCK) | (j > 0))
        else:
            valid = band
        valid4 = jnp.concatenate([valid] * Q_PER_KV, axis=0)
        outs = []
        for h in range(N_KV_HEADS):
            kt = kvt2[:, h * V7X_LANES:(h + 1) * V7X_LANES]
            vt = kvt2[:, (N_KV_HEADS + h) * V7X_LANES:(N_KV_HEADS + h + 1) * V7X_LANES]
            qh = q[rows, h * Q_PER_KV * HEAD_DIM:(h + 1) * Q_PER_KV * HEAD_DIM]
            q_lo, q_hi = qh[:, :V7X_LANES], qh[:, V7X_LANES:]
            zq = jnp.zeros_like(q_lo)
            q4 = jnp.concatenate(
                [jnp.where(lo_b, q_lo, zq), jnp.where(lo_b, zq, q_lo),
                 jnp.where(lo_b, q_hi, zq), jnp.where(lo_b, zq, q_hi)], axis=0)
            s = lax.dot_general(q4, kt, (((1,), (1,)), ((), ())),
                                preferred_element_type=_F32)
            slopes = [2.0 ** (-8.0 * (h * Q_PER_KV + g + 1) / N_Q_HEADS)
                      for g in range(Q_PER_KV)]
            alibi = jnp.concatenate([relf * (-sl) for sl in slopes], axis=0)
            sink = jnp.concatenate(
                [jnp.full((BLOCK, 1), sinks_ref[h * Q_PER_KV + g], _F32)
                 for g in range(Q_PER_KV)], axis=0)
            logits = jnp.where(valid4, s + alibi, _MASKED)
            m = jnp.maximum(jnp.max(logits, axis=-1, keepdims=True), sink)
            p = jnp.exp(logits - m)
            denom = jnp.sum(p, axis=-1, keepdims=True) + jnp.exp(sink - m)
            probs = (p * (1.0 / denom)).astype(_BF16)
            p4 = jnp.concatenate(
                [probs[g * BLOCK:(g + 1) * BLOCK, :] for g in range(Q_PER_KV)], axis=1)
            v_lo = jnp.where(lo_2b, vt, zeros_kv)
            v_hi = jnp.where(lo_2b, zeros_kv, vt)
            vpad = jnp.concatenate(
                [jnp.concatenate([v_lo, zeros_kv], axis=1),
                 jnp.concatenate([v_hi, zeros_kv], axis=1),
                 jnp.concatenate([zeros_kv, v_lo], axis=1),
                 jnp.concatenate([zeros_kv, v_hi], axis=1)], axis=0)
            outs.append(_dot(p4, vpad))
        attn_rows.append(jnp.concatenate(outs, axis=1))
    attn = jnp.concatenate(attn_rows, axis=0).astype(_BF16)

    z = _gelu(_dot(xn, win_ref[:, Z_OFF:GA_OFF]))
    u = z[:, :GMLP_WIDTH]
    v = z[:, GMLP_WIDTH:]
    mu = jnp.mean(v, axis=-1, keepdims=True)
    vc = v - mu
    var = jnp.mean(vc * vc, axis=-1, keepdims=True)
    vn = (vc * lax.rsqrt(var + LN_EPS) * lng_ref[...] + lnb_ref[...]).astype(_BF16)
    ti = lax.broadcasted_iota(jnp.int32, (BLOCK, BLOCK), 0)
    si = lax.broadcasted_iota(jnp.int32, (BLOCK, BLOCK), 1)
    causal = si <= ti
    f_cols = []
    for g in range(GMLP_GROUPS):
        cols = slice(g * BLOCK, (g + 1) * BLOCK)
        wg = jnp.where(causal, ws_ref[g], 0.0).astype(_BF16)
        rhs = jnp.concatenate(
            [vn[i * BLOCK:(i + 1) * BLOCK, cols] for i in range(nb)], axis=1)
        f_cols.append(_dot(wg, rhs) + bst_ref[:, g:g + 1])
    f = jnp.concatenate(
        [jnp.concatenate([fc[:, i * BLOCK:(i + 1) * BLOCK] for fc in f_cols], axis=1)
         for i in range(nb)], axis=0)
    gm = (u * f).astype(_BF16)

    br_a = _dot(attn, wa_ref[...])
    br_b = _dot(gm, wb_ref[...])
    g_a = _sigmoid(_dot(xn, win_ref[:, GA_OFF:GB_OFF]))
    g_b = _sigmoid(_dot(xn, win_ref[:, GB_OFF:IN_WIDTH]))
    merged = (g_a * br_a + g_b * br_b).astype(_BF16)
    mix = _dot(merged, wo_ref[...])
    h_ref[...] = x + _rms_norm(mix, gpost_ref[...])


def _ffn_kernel(h_ref, gpre_ref, wg_ref, wu_ref, wd_ref, gpost_ref, o_ref):
    h = h_ref[...]
    hn = _rms_norm(h, gpre_ref[...]).astype(_BF16)
    gate = _dot(hn, wg_ref[...])
    up = _dot(hn, wu_ref[...])
    act = (gate * _sigmoid(gate) * up).astype(_BF16)
    ff = _dot(act, wd_ref[...])
    o_ref[...] = h + _rms_norm(ff, gpost_ref[...])


def _resident(shape):
    zeros = (0,) * len(shape)
    return pl.BlockSpec(shape, lambda *_: zeros, pipeline_mode=pl.Buffered(1))


def _vmem_limit(resident_bytes, tile_rows, row_bytes):
    need = resident_bytes + tile_rows * row_bytes
    return int(min(need, V7X_VMEM_BYTES - 8 * 1024 * 1024))


def _mixer_call(x, gpre, win, sinks, lng, lnb, ws, bst, wa, wb, wo, gpost):
    b, s, d = x.shape
    tile = MIXER_TILE
    assert s % tile == 0 and tile % BLOCK == 0
    weights_bytes = 2 * (win.size + wa.size + wb.size + wo.size) + 4 * ws.size
    row_bytes = 4 * d * 4 + 4 * (IN_WIDTH + 4 * D_MODEL) * 3
    tok = pl.BlockSpec((None, tile, d), lambda i, j: (i, j, 0))
    return pl.pallas_call(
        functools.partial(_mixer_kernel, tile=tile),
        grid=(b, s // tile),
        in_specs=[
            tok,
            _resident(gpre.shape),
            _resident(win.shape),
            pl.BlockSpec(memory_space=pltpu.SMEM),
            _resident(lng.shape),
            _resident(lnb.shape),
            _resident(ws.shape),
            _resident(bst.shape),
            _resident(wa.shape),
            _resident(wb.shape),
            _resident(wo.shape),
            _resident(gpost.shape),
        ],
        out_specs=tok,
        out_shape=jax.ShapeDtypeStruct(x.shape, x.dtype),
        scratch_shapes=[pltpu.VMEM((BLOCK, 4 * V7X_LANES), _BF16)],
        compiler_params=pltpu.CompilerParams(
            dimension_semantics=("arbitrary", "arbitrary"),
            vmem_limit_bytes=_vmem_limit(weights_bytes, tile, row_bytes)),
        name="mixer",
    )(x, gpre, win, sinks, lng, lnb, ws, bst, wa, wb, wo, gpost)


def _ffn_call(h2d, gpre, wg, wu, wd, gpost):
    m, d = h2d.shape
    tile = FFN_TILE
    assert m % tile == 0
    weights_bytes = 2 * (wg.size + wu.size + wd.size)
    row_bytes = 4 * d * 4 + 4 * D_FF * 4
    row = pl.BlockSpec((tile, d), lambda i: (i, 0))
    return pl.pallas_call(
        _ffn_kernel,
        grid=(m // tile,),
        in_specs=[row, _resident(gpre.shape), _resident(wg.shape), _resident(wu.shape),
                  _resident(wd.shape), _resident(gpost.shape)],
        out_specs=row,
        out_shape=jax.ShapeDtypeStruct(h2d.shape, h2d.dtype),
        compiler_params=pltpu.CompilerParams(
            dimension_semantics=("arbitrary",),
            vmem_limit_bytes=_vmem_limit(weights_bytes, tile, row_bytes)),
        name="ffn",
    )(h2d, gpre, wg, wu, wd, gpost)


def kernel(x, norm_mix_pre, w_in, attn_sinks, gmlp_ln_g, gmlp_ln_b, gmlp_w_s, gmlp_b_s,
           w_attn_branch, w_gmlp_branch, w_out, norm_mix_post, norm_ffn_pre,
           w_ffn_gate, w_ffn_up, w_ffn_down, norm_ffn_post):
    b, s, d = x.shape
    depth = w_in.shape[0]
    h = x
    for l in range(depth):
        h = _mixer_call(
            h, norm_mix_pre[l][None, :], w_in[l].astype(_BF16), attn_sinks[l],
            gmlp_ln_g[l][None, :], gmlp_ln_b[l][None, :], gmlp_w_s[l],
            gmlp_b_s[l].T, w_attn_branch[l].astype(_BF16),
            w_gmlp_branch[l].astype(_BF16), w_out[l].astype(_BF16),
            norm_mix_post[l][None, :])
        h = _ffn_call(
            h.reshape(b * s, d), norm_ffn_pre[l][None, :], w_ffn_gate[l].astype(_BF16),
            w_ffn_up[l].astype(_BF16), w_ffn_down[l].astype(_BF16),
            norm_ffn_post[l][None, :]).reshape(b, s, d)
    return h
```

```python
import functools

import jax
import jax.numpy as jnp
import numpy as np
from jax import lax
from jax.experimental import pallas as pl
from jax.experimental.pallas import tpu as pltpu

D_MODEL = 1024
EPS = 1e-6
LN_EPS = 1e-5
N_Q_HEADS = 8
N_KV_HEADS = 2
HEAD_DIM = 64
Q_PER_KV = N_Q_HEADS // N_KV_HEADS
ATTN_WIDTH = N_Q_HEADS * HEAD_DIM
KV_WIDTH = N_KV_HEADS * HEAD_DIM
BLOCK = 128
GMLP_GROUPS = 4
GMLP_WIDTH = 512
D_FF = 2816
Q_OFF = 0
K_OFF = ATTN_WIDTH
V_OFF = K_OFF + KV_WIDTH
Z_OFF = V_OFF + KV_WIDTH
GA_OFF = Z_OFF + 2 * GMLP_WIDTH
GB_OFF = GA_OFF + D_MODEL
IN_WIDTH = GB_OFF + D_MODEL

V7X_LANES = 128
V7X_VMEM_BYTES = 64 * 1024 * 1024
MIXER_TILE = 256
FFN_TILE = 256

_BF16 = jnp.bfloat16
_F32 = jnp.float32
_SQRT_HALF = np.float32(np.sqrt(0.5))
_MASKED = -1e30


def _dot(a, b):
    return jnp.dot(a, b, preferred_element_type=_F32)


def _rms_norm(x, gain):
    y = x * lax.rsqrt(jnp.mean(x * x, axis=-1, keepdims=True) + EPS)
    return y * gain


def _gelu(x):
    return 0.5 * x * (1.0 + lax.erf(x * _SQRT_HALF))


def _sigmoid(x):
    return 1.0 / (1.0 + jnp.exp(-x))


def _mixer_kernel(x_ref, gpre_ref, win_ref, sinks_ref, lng_ref, lnb_ref, ws_ref,
                  bst_ref, wa_ref, wb_ref, wo_ref, gpost_ref, h_ref, kt_ref, vT_ref, *,
                  tile):
    nb = tile // BLOCK
    j = pl.program_id(1)

    @pl.when(j == 0)
    def _():
        kt_ref[...] = jnp.zeros_like(kt_ref)
        vT_ref[...] = jnp.zeros_like(vT_ref)

    x = x_ref[...]
    xn = _rms_norm(x, gpre_ref[...]).astype(_BF16)

    qkv = _dot(xn, win_ref[:, Q_OFF:Z_OFF])
    zg = _dot(xn, win_ref[:, Z_OFF:GA_OFF])

    q = (qkv[:, Q_OFF:K_OFF] * (HEAD_DIM ** -0.5)).astype(_BF16)
    lo = lax.broadcasted_iota(jnp.int32, (tile, V7X_LANES), 1) < HEAD_DIM
    kf = qkv[:, K_OFF:V_OFF]
    kr = pltpu.roll(kf, HEAD_DIM, 1)
    kt = jnp.concatenate(
        [jnp.where(lo, kf, kr), jnp.where(lo, kr, kf)], axis=1).astype(_BF16)
    vT = qkv[:, V_OFF:Z_OFF].T.astype(_BF16)
    kt_prev = kt_ref[...]
    vT_prev = vT_ref[...]
    kt_ref[...] = kt[tile - BLOCK:, :]
    vT_ref[...] = vT[:, tile - BLOCK:]

    ki = lax.broadcasted_iota(jnp.int32, (2 * BLOCK, BLOCK), 0)
    qi = lax.broadcasted_iota(jnp.int32, (2 * BLOCK, BLOCK), 1)
    rel = BLOCK + qi - ki
    band = (rel >= 0) & (rel < BLOCK)
    relf = rel.astype(_F32)
    lo_b = lax.broadcasted_iota(jnp.int32, (BLOCK, V7X_LANES), 1) < HEAD_DIM

    chains = [(i, h) for i in range(nb) for h in range(N_KV_HEADS)]
    kt2, vT2 = [], []
    for i in range(nb):
        rows = slice(i * BLOCK, (i + 1) * BLOCK)
        kt2.append(jnp.concatenate(
            [kt_prev if i == 0 else kt[(i - 1) * BLOCK:i * BLOCK, :], kt[rows, :]], axis=0))
        vT2.append(jnp.concatenate(
            [vT_prev if i == 0 else vT[:, (i - 1) * BLOCK:i * BLOCK], vT[:, rows]], axis=1))

    scores = []
    for i, h in chains:
        qh = q[i * BLOCK:(i + 1) * BLOCK,
               h * Q_PER_KV * HEAD_DIM:(h + 1) * Q_PER_KV * HEAD_DIM]
        q_lo, q_hi = qh[:, :V7X_LANES], qh[:, V7X_LANES:]
        zq = jnp.zeros_like(q_lo)
        q4 = jnp.concatenate(
            [jnp.where(lo_b, q_lo, zq), jnp.where(lo_b, zq, q_lo),
             jnp.where(lo_b, q_hi, zq), jnp.where(lo_b, zq, q_hi)], axis=0)
        scores.append(lax.dot_general(
            kt2[i][:, h * 2 * HEAD_DIM:(h + 1) * 2 * HEAD_DIM], q4,
            (((1,), (1,)), ((), ())), preferred_element_type=_F32))

    ga_pre = _dot(xn, win_ref[:, GA_OFF:GB_OFF])

    probs = []
    for (i, h), s in zip(chains, scores):
        valid = band & ((ki >= BLOCK) | (j > 0)) if i == 0 else band
        valid4 = jnp.concatenate([valid] * Q_PER_KV, axis=1)
        slopes = [2.0 ** (-8.0 * (h * Q_PER_KV + g + 1) / N_Q_HEADS)
                  for g in range(Q_PER_KV)]
        alibi = jnp.concatenate([relf * (-sl) for sl in slopes], axis=1)
        sink = jnp.concatenate(
            [jnp.full((1, BLOCK), sinks_ref[h * Q_PER_KV + g], _F32)
             for g in range(Q_PER_KV)], axis=1)
        logits = jnp.where(valid4, s + alibi, _MASKED)
        m = jnp.maximum(jnp.max(logits, axis=0, keepdims=True), sink)
        p = jnp.exp(logits - m)
        denom = jnp.sum(p, axis=0, keepdims=True) + jnp.exp(sink - m)
        probs.append((p * (1.0 / denom)).astype(_BF16))

    gb_pre = _dot(xn, win_ref[:, GB_OFF:IN_WIDTH])

    outs = []
    for (i, h), pr in zip(chains, probs):
        outs.append(_dot(vT2[i][h * HEAD_DIM:(h + 1) * HEAD_DIM, :], pr))
    attn_t = jnp.concatenate(
        [jnp.concatenate(
            [outs[i * N_KV_HEADS + h][:, g * BLOCK:(g + 1) * BLOCK]
             for h in range(N_KV_HEADS) for g in range(Q_PER_KV)], axis=0)
         for i in range(nb)], axis=1).astype(_BF16)
    br_a = lax.dot_general(attn_t, wa_ref[...], (((0,), (0,)), ((), ())),
                           preferred_element_type=_F32)

    z = _gelu(zg)
    u = z[:, :GMLP_WIDTH]
    v = z[:, GMLP_WIDTH:]
    mu = jnp.mean(v, axis=-1, keepdims=True)
    vc = v - mu
    var = jnp.mean(vc * vc, axis=-1, keepdims=True)
    vn = (vc * lax.rsqrt(var + LN_EPS) * lng_ref[...] + lnb_ref[...]).astype(_BF16)
    ti = lax.broadcasted_iota(jnp.int32, (BLOCK, BLOCK), 0)
    si = lax.broadcasted_iota(jnp.int32, (BLOCK, BLOCK), 1)
    causal = si <= ti
    f_cols = []
    for g in range(GMLP_GROUPS):
        cols = slice(g * BLOCK, (g + 1) * BLOCK)
        wg = jnp.where(causal, ws_ref[g], 0.0).astype(_BF16)
        rhs = jnp.concatenate(
            [vn[i * BLOCK:(i + 1) * BLOCK, cols] for i in range(nb)], axis=1)
        f_cols.append(_dot(wg, rhs) + bst_ref[:, g:g + 1])
    f = jnp.concatenate(
        [jnp.concatenate([fc[:, i * BLOCK:(i + 1) * BLOCK] for fc in f_cols], axis=1)
         for i in range(nb)], axis=0)
    gm = (u * f).astype(_BF16)
    br_b = _dot(gm, wb_ref[...])

    merged = (_sigmoid(ga_pre) * br_a + _sigmoid(gb_pre) * br_b).astype(_BF16)
    mix = _dot(merged, wo_ref[...])
    h_ref[...] = x + _rms_norm(mix, gpost_ref[...])


def _ffn_kernel(h_ref, gpre_ref, wg_ref, wu_ref, wd_ref, gpost_ref, o_ref):
    h = h_ref[...]
    hn = _rms_norm(h, gpre_ref[...]).astype(_BF16)
    gate = _dot(hn, wg_ref[...])
    up = _dot(hn, wu_ref[...])
    act = (gate * _sigmoid(gate) * up).astype(_BF16)
    ff = _dot(act, wd_ref[...])
    o_ref[...] = h + _rms_norm(ff, gpost_ref[...])


def _resident(shape):
    zeros = (0,) * len(shape)
    return pl.BlockSpec(shape, lambda *_: zeros, pipeline_mode=pl.Buffered(1))


def _vmem_limit(resident_bytes, tile_rows, row_bytes):
    need = resident_bytes + tile_rows * row_bytes
    return int(min(need, V7X_VMEM_BYTES - 8 * 1024 * 1024))


def _mixer_call(x, gpre, win, sinks, lng, lnb, ws, bst, wa, wb, wo, gpost):
    b, s, d = x.shape
    tile = MIXER_TILE
    assert s % tile == 0 and tile % BLOCK == 0
    weights_bytes = 2 * (win.size + wa.size + wb.size + wo.size) + 4 * ws.size
    row_bytes = 4 * d * 4 + 4 * (IN_WIDTH + 4 * D_MODEL) * 3
    tok = pl.BlockSpec((None, tile, d), lambda i, j: (i, j, 0))
    return pl.pallas_call(
        functools.partial(_mixer_kernel, tile=tile),
        grid=(b, s // tile),
        in_specs=[
            tok,
            _resident(gpre.shape),
            _resident(win.shape),
            pl.BlockSpec(memory_space=pltpu.SMEM),
            _resident(lng.shape),
            _resident(lnb.shape),
            _resident(ws.shape),
            _resident(bst.shape),
            _resident(wa.shape),
            _resident(wb.shape),
            _resident(wo.shape),
            _resident(gpost.shape),
        ],
        out_specs=tok,
        out_shape=jax.ShapeDtypeStruct(x.shape, x.dtype),
        scratch_shapes=[pltpu.VMEM((BLOCK, 2 * V7X_LANES), _BF16),
                        pltpu.VMEM((KV_WIDTH, BLOCK), _BF16)],
        compiler_params=pltpu.CompilerParams(
            dimension_semantics=("arbitrary", "arbitrary"),
            vmem_limit_bytes=_vmem_limit(weights_bytes, tile, row_bytes)),
        name="mixer",
    )(x, gpre, win, sinks, lng, lnb, ws, bst, wa, wb, wo, gpost)


def _ffn_call(h2d, gpre, wg, wu, wd, gpost):
    m, d = h2d.shape
    tile = FFN_TILE
    assert m % tile == 0
    weights_bytes = 2 * (wg.size + wu.size + wd.size)
    row_bytes = 4 * d * 4 + 4 * D_FF * 4
    row = pl.BlockSpec((tile, d), lambda i: (i, 0))
    return pl.pallas_call(
        _ffn_kernel,
        grid=(m // tile,),
        in_specs=[row, _resident(gpre.shape), _resident(wg.shape), _resident(wu.shape),
                  _resident(wd.shape), _resident(gpost.shape)],
        out_specs=row,
        out_shape=jax.ShapeDtypeStruct(h2d.shape, h2d.dtype),
        compiler_params=pltpu.CompilerParams(
            dimension_semantics=("arbitrary",),
            vmem_limit_bytes=_vmem_limit(weights_bytes, tile, row_bytes)),
        name="ffn",
    )(h2d, gpre, wg, wu, wd, gpost)


def kernel(x, norm_mix_pre, w_in, attn_sinks, gmlp_ln_g, gmlp_ln_b, gmlp_w_s, gmlp_b_s,
           w_attn_branch, w_gmlp_branch, w_out, norm_mix_post, norm_ffn_pre,
           w_ffn_gate, w_ffn_up, w_ffn_down, norm_ffn_post):
    b, s, d = x.shape
    depth = w_in.shape[0]
    h = x
    for l in range(depth):
        h = _mixer_call(
            h, norm_mix_pre[l][None, :], w_in[l].astype(_BF16), attn_sinks[l],
            gmlp_ln_g[l][None, :], gmlp_ln_b[l][None, :], gmlp_w_s[l],
            gmlp_b_s[l].T, w_attn_branch[l].astype(_BF16),
            w_gmlp_branch[l].astype(_BF16), w_out[l].astype(_BF16),
            norm_mix_post[l][None, :])
        h = _ffn_call(
            h.reshape(b * s, d), norm_ffn_pre[l][None, :], w_ffn_gate[l].astype(_BF16),
            w_ffn_up[l].astype(_BF16), w_ffn_down[l].astype(_BF16),
            norm_ffn_post[l][None, :]).reshape(b, s, d)
    return h
```

```python
import functools

import jax
import jax.numpy as jnp
import numpy as np
from jax import lax
from jax.experimental import pallas as pl
from jax.experimental.pallas import tpu as pltpu

D_MODEL = 1024
EPS = 1e-6
LN_EPS = 1e-5
N_Q_HEADS = 8
N_KV_HEADS = 2
HEAD_DIM = 64
Q_PER_KV = N_Q_HEADS // N_KV_HEADS
ATTN_WIDTH = N_Q_HEADS * HEAD_DIM
KV_WIDTH = N_KV_HEADS * HEAD_DIM
BLOCK = 128
GMLP_GROUPS = 4
GMLP_WIDTH = 512
D_FF = 2816
Q_OFF = 0
K_OFF = ATTN_WIDTH
V_OFF = K_OFF + KV_WIDTH
Z_OFF = V_OFF + KV_WIDTH
GA_OFF = Z_OFF + 2 * GMLP_WIDTH
GB_OFF = GA_OFF + D_MODEL
IN_WIDTH = GB_OFF + D_MODEL

V7X_LANES = 128
V7X_VMEM_BYTES = 64 * 1024 * 1024
MIXER_TILE = 1024
MIXER_SUB = 256
MIXER_STAGGER = 0
FFN_TILE = 1024
FFN_SUB = 256

_BF16 = jnp.bfloat16
_F32 = jnp.float32
_SQRT_HALF = np.float32(np.sqrt(0.5))
_MASKED = -1e30


def _dot(a, b):
    return jnp.dot(a, b, preferred_element_type=_F32)


def _rms_norm(x, gain):
    y = x * lax.rsqrt(jnp.mean(x * x, axis=-1, keepdims=True) + EPS)
    return y * gain


def _gelu(x):
    return 0.5 * x * (1.0 + lax.erf(x * _SQRT_HALF))


def _sigmoid(x):
    return 1.0 / (1.0 + jnp.exp(-x))


def _interleave(generators):
    live = list(generators)
    while live:
        for gen in list(live):
            try:
                next(gen)
            except StopIteration:
                live.remove(gen)


def _emit_pipelined(phases_per_sub, stagger):
    slots = sorted((k + t * stagger, t, k)
                   for t, phases in enumerate(phases_per_sub) for k in range(len(phases)))
    for _, t, k in slots:
        for item in phases_per_sub[t][k]:
            item()


def _mixer_items(t, r0, sub, tile, states, x_ref, gpre_ref, win_ref, sinks_ref, lng_ref,
                 lnb_ref, ws_ref, bst_ref, wa_ref, wb_ref, wo_ref, gpost_ref, h_ref,
                 kt_ref, vT_ref):
    nb = sub // BLOCK
    st = states[t]
    chains = [(i, h) for i in range(nb) for h in range(N_KV_HEADS)]
    by_name = {}

    def add(name):
        def register(fn):
            by_name[name] = fn
            return fn
        return register

    @add("xn")
    def _():
        st["x"] = x_ref[pl.ds(r0, sub), :]
        st["xn"] = _rms_norm(st["x"], gpre_ref[...]).astype(_BF16)

    @add("qkv")
    def _():
        st["qkv"] = _dot(st["xn"], win_ref[:, Q_OFF:Z_OFF])

    @add("prep")
    def _():
        qkv = st["qkv"]
        q = (qkv[:, Q_OFF:K_OFF] * (HEAD_DIM ** -0.5)).astype(_BF16)
        lo = lax.broadcasted_iota(jnp.int32, (sub, V7X_LANES), 1) < HEAD_DIM
        kf = qkv[:, K_OFF:V_OFF]
        kr = pltpu.roll(kf, HEAD_DIM, 1)
        kt = jnp.concatenate(
            [jnp.where(lo, kf, kr), jnp.where(lo, kr, kf)], axis=1).astype(_BF16)
        vT = qkv[:, V_OFF:Z_OFF].T.astype(_BF16)
        slot = lax.rem(pl.program_id(1), 2)
        if t == 0:
            kt_prev, vT_prev = kt_ref[slot], vT_ref[slot]
        else:
            kt_prev, vT_prev = states[t - 1]["kt_last"], states[t - 1]["vT_last"]
        st["kt_last"], st["vT_last"] = kt[sub - BLOCK:, :], vT[:, sub - BLOCK:]
        if r0 + sub == tile:
            kt_ref[1 - slot] = st["kt_last"]
            vT_ref[1 - slot] = st["vT_last"]
        lo_b = lax.broadcasted_iota(jnp.int32, (BLOCK, V7X_LANES), 1) < HEAD_DIM
        for i, h in chains:
            rows = slice(i * BLOCK, (i + 1) * BLOCK)
            prev = slice((i - 1) * BLOCK, i * BLOCK)
            cols = slice(h * 2 * HEAD_DIM, (h + 1) * 2 * HEAD_DIM)
            st["kt2", i, h] = jnp.concatenate(
                [kt_prev[:, cols] if i == 0 else kt[prev, cols], kt[rows, cols]], axis=0)
            hd = slice(h * HEAD_DIM, (h + 1) * HEAD_DIM)
            st["vT2", i, h] = jnp.concatenate(
                [vT_prev[hd, :] if i == 0 else vT[hd, prev], vT[hd, rows]], axis=1)
            qh = q[rows, h * Q_PER_KV * HEAD_DIM:(h + 1) * Q_PER_KV * HEAD_DIM]
            q_lo, q_hi = qh[:, :V7X_LANES], qh[:, V7X_LANES:]
            zq = jnp.zeros_like(q_lo)
            st["q4", i, h] = jnp.concatenate(
                [jnp.where(lo_b, q_lo, zq), jnp.where(lo_b, zq, q_lo),
                 jnp.where(lo_b, q_hi, zq), jnp.where(lo_b, zq, q_hi)], axis=0)

    @add("zg")
    def _():
        st["zg"] = _dot(st["xn"], win_ref[:, Z_OFF:GA_OFF])

    @add("gelu_u")
    def _():
        st["u"] = _gelu(st["zg"][:, :GMLP_WIDTH])

    @add("gelu_v")
    def _():
        v = _gelu(st["zg"][:, GMLP_WIDTH:])
        mu = jnp.mean(v, axis=-1, keepdims=True)
        vc = v - mu
        var = jnp.mean(vc * vc, axis=-1, keepdims=True)
        st["vn"] = (vc * lax.rsqrt(var + LN_EPS) * lng_ref[...] + lnb_ref[...]).astype(_BF16)

    for c, (i, h) in enumerate(chains):
        @add(f"qk{c}")
        def _(i=i, h=h):
            st["s", i, h] = lax.dot_general(
                st["kt2", i, h], st["q4", i, h], (((1,), (1,)), ((), ())),
                preferred_element_type=_F32)

    for c, (i, h) in enumerate(chains):
        @add(f"softmax{c}")
        def _(i=i, h=h):
            ki = lax.broadcasted_iota(jnp.int32, (2 * BLOCK, BLOCK), 0)
            qi = lax.broadcasted_iota(jnp.int32, (2 * BLOCK, BLOCK), 1)
            rel = BLOCK + qi - ki
            valid = (rel >= 0) & (rel < BLOCK)
            if r0 == 0 and i == 0:
                valid = valid & ((ki >= BLOCK) | (pl.program_id(1) > 0))
            relf = rel.astype(_F32)
            valid4 = jnp.concatenate([valid] * Q_PER_KV, axis=1)
            slopes = [2.0 ** (-8.0 * (h * Q_PER_KV + g + 1) / N_Q_HEADS)
                      for g in range(Q_PER_KV)]
            alibi = jnp.concatenate([relf * (-sl) for sl in slopes], axis=1)
            sink = jnp.concatenate(
                [jnp.full((1, BLOCK), sinks_ref[h * Q_PER_KV + g], _F32)
                 for g in range(Q_PER_KV)], axis=1)
            logits = jnp.where(valid4, st["s", i, h] + alibi, _MASKED)
            m = jnp.maximum(jnp.max(logits, axis=0, keepdims=True), sink)
            p = jnp.exp(logits - m)
            denom = jnp.sum(p, axis=0, keepdims=True) + jnp.exp(sink - m)
            st["p", i, h] = (p * (1.0 / denom)).astype(_BF16)

    @add("ga")
    def _():
        st["ga"] = _dot(st["xn"], win_ref[:, GA_OFF:GB_OFF])

    @add("sig_a")
    def _():
        st["ga"] = _sigmoid(st["ga"])

    @add("gb")
    def _():
        st["gb"] = _dot(st["xn"], win_ref[:, GB_OFF:IN_WIDTH])

    @add("sig_b")
    def _():
        st["gb"] = _sigmoid(st["gb"])

    @add("spatial")
    def _():
        ti = lax.broadcasted_iota(jnp.int32, (BLOCK, BLOCK), 0)
        si = lax.broadcasted_iota(jnp.int32, (BLOCK, BLOCK), 1)
        causal = si <= ti
        vn = st["vn"]
        f_cols = []
        for g in range(GMLP_GROUPS):
            cols = slice(g * BLOCK, (g + 1) * BLOCK)
            wg = jnp.where(causal, ws_ref[g], 0.0).astype(_BF16)
            rhs = jnp.concatenate(
                [vn[i * BLOCK:(i + 1) * BLOCK, cols] for i in range(nb)], axis=1)
            f_cols.append(_dot(wg, rhs) + bst_ref[:, g:g + 1])
        st["f_cols"] = f_cols

    @add("gm")
    def _():
        f = jnp.concatenate(
            [jnp.concatenate([fc[:, i * BLOCK:(i + 1) * BLOCK] for fc in st["f_cols"]], axis=1)
             for i in range(nb)], axis=0)
        st["gm"] = (st["u"] * f).astype(_BF16)

    for c, (i, h) in enumerate(chains):
        @add(f"pv{c}")
        def _(i=i, h=h):
            st["o", i, h] = _dot(st["vT2", i, h], st["p", i, h])

    @add("br_b")
    def _():
        st["br_b"] = _dot(st["gm"], wb_ref[...])

    @add("br_a")
    def _():
        attn_t = jnp.concatenate(
            [jnp.concatenate(
                [st["o", i, h][:, g * BLOCK:(g + 1) * BLOCK]
                 for h in range(N_KV_HEADS) for g in range(Q_PER_KV)], axis=0)
             for i in range(nb)], axis=1).astype(_BF16)
        st["br_a"] = lax.dot_general(attn_t, wa_ref[...], (((0,), (0,)), ((), ())),
                                     preferred_element_type=_F32)

    @add("merged")
    def _():
        st["merged"] = (st["ga"] * st["br_a"] + st["gb"] * st["br_b"]).astype(_BF16)

    @add("mix")
    def _():
        st["mix"] = _dot(st["merged"], wo_ref[...])

    @add("final")
    def _():
        h_ref[pl.ds(r0, sub), :] = st["x"] + _rms_norm(st["mix"], gpost_ref[...])

    n = range(len(chains))
    phases = [["xn"], ["qkv"], ["zg", "prep"], [f"qk{c}" for c in n],
              ["gelu_u", "gelu_v", "ga"], [f"softmax{c}" for c in n] + ["gb"],
              ["spatial"], [f"pv{c}" for c in n], ["gm", "br_b"], ["br_a"],
              ["sig_a", "sig_b", "merged", "mix"], ["final"]]
    assert sorted(sum(phases, [])) == sorted(by_name)
    return [[by_name[name] for name in phase] for phase in phases]


def _mixer_kernel(*refs, tile, sub):
    kt_ref, vT_ref = refs[-2:]

    @pl.when(pl.program_id(1) == 0)
    def _():
        kt_ref[...] = jnp.zeros_like(kt_ref)
        vT_ref[...] = jnp.zeros_like(vT_ref)

    n_sub = tile // sub
    states = [dict() for _ in range(n_sub)]
    _emit_pipelined([_mixer_items(t, t * sub, sub, tile, states, *refs)
                     for t in range(n_sub)], MIXER_STAGGER)


def _ffn_rows(rows, h_ref, gpre_ref, wg_ref, wu_ref, wd_ref, gpost_ref, o_ref):
    h = h_ref[rows, :]
    hn = _rms_norm(h, gpre_ref[...]).astype(_BF16)
    yield
    gate = _dot(hn, wg_ref[...])
    yield
    up = _dot(hn, wu_ref[...])
    yield
    act = (gate * _sigmoid(gate) * up).astype(_BF16)
    ff = _dot(act, wd_ref[...])
    yield
    o_ref[rows, :] = h + _rms_norm(ff, gpost_ref[...])


def _ffn_kernel(*refs, tile, sub):
    _interleave(_ffn_rows(pl.ds(r, sub), *refs) for r in range(0, tile, sub))


def _resident(shape):
    zeros = (0,) * len(shape)
    return pl.BlockSpec(shape, lambda *_: zeros, pipeline_mode=pl.Buffered(1))


def _vmem_limit(resident_bytes, tile_rows, row_bytes):
    need = resident_bytes + tile_rows * row_bytes
    return int(min(need, V7X_VMEM_BYTES - 8 * 1024 * 1024))


def _mixer_call(x, gpre, win, sinks, lng, lnb, ws, bst, wa, wb, wo, gpost):
    b, s, d = x.shape
    tile, sub = MIXER_TILE, MIXER_SUB
    assert s % tile == 0 and tile % sub == 0 and sub % BLOCK == 0
    weights_bytes = 2 * (win.size + wa.size + wb.size + wo.size) + 4 * ws.size
    row_bytes = 4 * d * 4 + 4 * (IN_WIDTH + 4 * D_MODEL) * 3
    tok = pl.BlockSpec((None, tile, d), lambda i, j: (i, j, 0))
    return pl.pallas_call(
        functools.partial(_mixer_kernel, tile=tile, sub=sub),
        grid=(b, s // tile),
        in_specs=[
            tok,
            _resident(gpre.shape),
            _resident(win.shape),
            pl.BlockSpec(memory_space=pltpu.SMEM),
            _resident(lng.shape),
            _resident(lnb.shape),
            _resident(ws.shape),
            _resident(bst.shape),
            _resident(wa.shape),
            _resident(wb.shape),
            _resident(wo.shape),
            _resident(gpost.shape),
        ],
        out_specs=tok,
        out_shape=jax.ShapeDtypeStruct(x.shape, x.dtype),
        scratch_shapes=[pltpu.VMEM((2, BLOCK, 2 * V7X_LANES), _BF16),
                        pltpu.VMEM((2, KV_WIDTH, BLOCK), _BF16)],
        compiler_params=pltpu.CompilerParams(
            dimension_semantics=("arbitrary", "arbitrary"),
            vmem_limit_bytes=_vmem_limit(weights_bytes, tile, row_bytes)),
        name="mixer",
    )(x, gpre, win, sinks, lng, lnb, ws, bst, wa, wb, wo, gpost)


def _ffn_call(h2d, gpre, wg, wu, wd, gpost):
    m, d = h2d.shape
    tile, sub = FFN_TILE, FFN_SUB
    assert m % tile == 0 and tile % sub == 0
    weights_bytes = 2 * (wg.size + wu.size + wd.size)
    row_bytes = 4 * d * 4 + 4 * D_FF * 4
    row = pl.BlockSpec((tile, d), lambda i: (i, 0))
    return pl.pallas_call(
        functools.partial(_ffn_kernel, tile=tile, sub=sub),
        grid=(m // tile,),
        in_specs=[row, _resident(gpre.shape), _resident(wg.shape), _resident(wu.shape),
                  _resident(wd.shape), _resident(gpost.shape)],
        out_specs=row,
        out_shape=jax.ShapeDtypeStruct(h2d.shape, h2d.dtype),
        compiler_params=pltpu.CompilerParams(
            dimension_semantics=("arbitrary",),
            vmem_limit_bytes=_vmem_limit(weights_bytes, tile, row_bytes)),
        name="ffn",
    )(h2d, gpre, wg, wu, wd, gpost)


def kernel(x, norm_mix_pre, w_in, attn_sinks, gmlp_ln_g, gmlp_ln_b, gmlp_w_s, gmlp_b_s,
           w_attn_branch, w_gmlp_branch, w_out, norm_mix_post, norm_ffn_pre,
           w_ffn_gate, w_ffn_up, w_ffn_down, norm_ffn_post):
    b, s, d = x.shape
    depth = w_in.shape[0]
    h = x
    for l in range(depth):
        h = _mixer_call(
            h, norm_mix_pre[l][None, :], w_in[l].astype(_BF16), attn_sinks[l],
            gmlp_ln_g[l][None, :], gmlp_ln_b[l][None, :], gmlp_w_s[l],
            gmlp_b_s[l].T, w_attn_branch[l].astype(_BF16),
            w_gmlp_branch[l].astype(_BF16), w_out[l].astype(_BF16),
            norm_mix_post[l][None, :])
        h = _ffn_call(
            h.reshape(b * s, d), norm_ffn_pre[l][None, :], w_ffn_gate[l].astype(_BF16),
            w_ffn_up[l].astype(_BF16), w_ffn_down[l].astype(_BF16),
            norm_ffn_post[l][None, :]).reshape(b, s, d)
    return h
```

```python
import functools

import jax
import jax.numpy as jnp
import numpy as np
from jax import lax
from jax.experimental import pallas as pl
from jax.experimental.pallas import tpu as pltpu

D_MODEL = 1024
EPS = 1e-6
LN_EPS = 1e-5
N_Q_HEADS = 8
N_KV_HEADS = 2
HEAD_DIM = 64
Q_PER_KV = N_Q_HEADS // N_KV_HEADS
ATTN_WIDTH = N_Q_HEADS * HEAD_DIM
KV_WIDTH = N_KV_HEADS * HEAD_DIM
BLOCK = 128
GMLP_GROUPS = 4
GMLP_WIDTH = 512
D_FF = 2816
Q_OFF = 0
K_OFF = ATTN_WIDTH
V_OFF = K_OFF + KV_WIDTH
Z_OFF = V_OFF + KV_WIDTH
GA_OFF = Z_OFF + 2 * GMLP_WIDTH
GB_OFF = GA_OFF + D_MODEL
IN_WIDTH = GB_OFF + D_MODEL

V7X_LANES = 128
V7X_VMEM_BYTES = 64 * 1024 * 1024
MIXER_TILE = 1024
MIXER_SUB = 256
MIXER_STAGGER = 0
FFN_SUBS = (256, 256, 256, 256)

_BF16 = jnp.bfloat16
_F32 = jnp.float32
_SQRT_HALF = np.float32(np.sqrt(0.5))
_MASKED = -1e30


def _dot(a, b):
    return jnp.dot(a, b, preferred_element_type=_F32)


def _rms_norm(x, gain):
    y = x * lax.rsqrt(jnp.mean(x * x, axis=-1, keepdims=True) + EPS)
    return y * gain


def _gelu(x):
    return 0.5 * x * (1.0 + lax.erf(x * _SQRT_HALF))


def _sigmoid(x):
    return 1.0 / (1.0 + jnp.exp(-x))


def _interleave(generators):
    live = list(generators)
    while live:
        for gen in list(live):
            try:
                next(gen)
            except StopIteration:
                live.remove(gen)


def _emit_pipelined(phases_per_sub, stagger):
    slots = sorted((k + t * stagger, t, k)
                   for t, phases in enumerate(phases_per_sub) for k in range(len(phases)))
    for _, t, k in slots:
        for item in phases_per_sub[t][k]:
            item()


def _attention_consts(sinks_ref):
    ki = lax.broadcasted_iota(jnp.int32, (2 * BLOCK, BLOCK), 0)
    qi = lax.broadcasted_iota(jnp.int32, (2 * BLOCK, BLOCK), 1)
    rel = BLOCK + qi - ki
    band = (rel >= 0) & (rel < BLOCK)
    first = band & ((ki >= BLOCK) | (pl.program_id(1) > 0))
    relf = rel.astype(_F32)
    consts = {"valid": jnp.concatenate([band] * Q_PER_KV, axis=1),
              "valid_first": jnp.concatenate([first] * Q_PER_KV, axis=1)}
    for h in range(N_KV_HEADS):
        heads = range(h * Q_PER_KV, (h + 1) * Q_PER_KV)
        consts["alibi", h] = jnp.concatenate(
            [relf * -(2.0 ** (-8.0 * (hq + 1) / N_Q_HEADS)) for hq in heads], axis=1)
        consts["sink", h] = sinks_ref[h:h + 1, :]
    return consts


def _mixer_items(t, r0, sub, tile, states, consts, x_ref, gpre_ref, win_ref, sinks_ref,
                 lng_ref, lnb_ref, ws_ref, bst_ref, wa_ref, wb_ref, wo_ref, gpost_ref,
                 h_ref, kt_ref, vT_ref):
    nb = sub // BLOCK
    st = states[t]
    chains = [(i, h) for i in range(nb) for h in range(N_KV_HEADS)]
    by_name = {}

    def add(name):
        def register(fn):
            by_name[name] = fn
            return fn
        return register

    @add("xn")
    def _():
        st["x"] = x_ref[pl.ds(r0, sub), :]
        st["xn"] = _rms_norm(st["x"], gpre_ref[...]).astype(_BF16)

    @add("qkv")
    def _():
        st["qkv"] = _dot(st["xn"], win_ref[:, Q_OFF:Z_OFF])

    @add("prep")
    def _():
        qkv = st["qkv"]
        q = (qkv[:, Q_OFF:K_OFF] * (HEAD_DIM ** -0.5)).astype(_BF16)
        lo = lax.broadcasted_iota(jnp.int32, (sub, V7X_LANES), 1) < HEAD_DIM
        kf = qkv[:, K_OFF:V_OFF]
        kr = pltpu.roll(kf, HEAD_DIM, 1)
        kt = jnp.concatenate(
            [jnp.where(lo, kf, kr), jnp.where(lo, kr, kf)], axis=1).astype(_BF16)
        vT = qkv[:, V_OFF:Z_OFF].T.astype(_BF16)
        slot = lax.rem(pl.program_id(1), 2)
        if t == 0:
            kt_prev, vT_prev = kt_ref[slot], vT_ref[slot]
        else:
            kt_prev, vT_prev = states[t - 1]["kt_last"], states[t - 1]["vT_last"]
        st["kt_last"], st["vT_last"] = kt[sub - BLOCK:, :], vT[:, sub - BLOCK:]
        if r0 + sub == tile:
            kt_ref[1 - slot] = st["kt_last"]
            vT_ref[1 - slot] = st["vT_last"]
        lo_b = lax.broadcasted_iota(jnp.int32, (BLOCK, V7X_LANES), 1) < HEAD_DIM
        for i, h in chains:
            rows = slice(i * BLOCK, (i + 1) * BLOCK)
            prev = slice((i - 1) * BLOCK, i * BLOCK)
            cols = slice(h * 2 * HEAD_DIM, (h + 1) * 2 * HEAD_DIM)
            st["kt2", i, h] = jnp.concatenate(
                [kt_prev[:, cols] if i == 0 else kt[prev, cols], kt[rows, cols]], axis=0)
            hd = slice(h * HEAD_DIM, (h + 1) * HEAD_DIM)
            st["vT2", i, h] = jnp.concatenate(
                [vT_prev[hd, :] if i == 0 else vT[hd, prev], vT[hd, rows]], axis=1)
            qh = q[rows, h * Q_PER_KV * HEAD_DIM:(h + 1) * Q_PER_KV * HEAD_DIM]
            q_lo, q_hi = qh[:, :V7X_LANES], qh[:, V7X_LANES:]
            zq = jnp.zeros_like(q_lo)
            st["q4", i, h] = jnp.concatenate(
                [jnp.where(lo_b, q_lo, zq), jnp.where(lo_b, zq, q_lo),
                 jnp.where(lo_b, q_hi, zq), jnp.where(lo_b, zq, q_hi)], axis=0)

    @add("zg")
    def _():
        st["zg"] = _dot(st["xn"], win_ref[:, Z_OFF:GA_OFF])

    @add("gelu_u")
    def _():
        st["u"] = _gelu(st["zg"][:, :GMLP_WIDTH])

    @add("gelu_v")
    def _():
        v = _gelu(st["zg"][:, GMLP_WIDTH:])
        mu = jnp.mean(v, axis=-1, keepdims=True)
        vc = v - mu
        var = jnp.mean(vc * vc, axis=-1, keepdims=True)
        st["vn"] = (vc * lax.rsqrt(var + LN_EPS) * lng_ref[...] + lnb_ref[...]).astype(_BF16)

    for c, (i, h) in enumerate(chains):
        @add(f"qk{c}")
        def _(i=i, h=h):
            st["s", i, h] = lax.dot_general(
                st["kt2", i, h], st["q4", i, h], (((1,), (1,)), ((), ())),
                preferred_element_type=_F32)

    for c, (i, h) in enumerate(chains):
        @add(f"softmax{c}")
        def _(i=i, h=h):
            valid = consts["valid_first" if r0 == 0 and i == 0 else "valid"]
            sink = consts["sink", h]
            logits = jnp.where(valid, st["s", i, h] + consts["alibi", h], _MASKED)
            m = jnp.maximum(jnp.max(logits, axis=0, keepdims=True), sink)
            p = jnp.exp(logits - m)
            denom = jnp.sum(p, axis=0, keepdims=True) + jnp.exp(sink - m)
            st["p", i, h] = p.astype(_BF16)
            st["rdenom", i, h] = 1.0 / denom

    @add("ga")
    def _():
        st["ga"] = _dot(st["xn"], win_ref[:, GA_OFF:GB_OFF])

    @add("sig_a")
    def _():
        st["ga"] = _sigmoid(st["ga"])

    @add("gb")
    def _():
        st["gb"] = _dot(st["xn"], win_ref[:, GB_OFF:IN_WIDTH])

    @add("sig_b")
    def _():
        st["gb"] = _sigmoid(st["gb"])

    @add("spatial")
    def _():
        ti = lax.broadcasted_iota(jnp.int32, (BLOCK, BLOCK), 0)
        si = lax.broadcasted_iota(jnp.int32, (BLOCK, BLOCK), 1)
        causal = si <= ti
        vn = st["vn"]
        f_cols = []
        for g in range(GMLP_GROUPS):
            cols = slice(g * BLOCK, (g + 1) * BLOCK)
            wg = jnp.where(causal, ws_ref[g], 0.0).astype(_BF16)
            rhs = jnp.concatenate(
                [vn[i * BLOCK:(i + 1) * BLOCK, cols] for i in range(nb)], axis=1)
            f_cols.append(_dot(wg, rhs) + bst_ref[:, g:g + 1])
        st["f_cols"] = f_cols

    @add("gm")
    def _():
        f = jnp.concatenate(
            [jnp.concatenate([fc[:, i * BLOCK:(i + 1) * BLOCK] for fc in st["f_cols"]], axis=1)
             for i in range(nb)], axis=0)
        st["gm"] = (st["u"] * f).astype(_BF16)

    for c, (i, h) in enumerate(chains):
        @add(f"pv{c}")
        def _(i=i, h=h):
            st["o", i, h] = (_dot(st["vT2", i, h], st["p", i, h])
                             * st["rdenom", i, h])

    @add("br_b")
    def _():
        st["br_b"] = _dot(st["gm"], wb_ref[...])

    @add("br_a")
    def _():
        attn_t = jnp.concatenate(
            [jnp.concatenate(
                [st["o", i, h][:, g * BLOCK:(g + 1) * BLOCK]
                 for h in range(N_KV_HEADS) for g in range(Q_PER_KV)], axis=0)
             for i in range(nb)], axis=1).astype(_BF16)
        st["br_a"] = lax.dot_general(attn_t, wa_ref[...], (((0,), (0,)), ((), ())),
                                     preferred_element_type=_F32)

    @add("merged")
    def _():
        st["merged"] = (st["ga"] * st["br_a"] + st["gb"] * st["br_b"]).astype(_BF16)

    @add("mix")
    def _():
        st["mix"] = _dot(st["merged"], wo_ref[...])

    @add("final")
    def _():
        h_ref[pl.ds(r0, sub), :] = st["x"] + _rms_norm(st["mix"], gpost_ref[...])

    n = range(len(chains))
    phases = [["xn"], ["qkv"], ["zg", "prep"], [f"qk{c}" for c in n],
              ["gelu_u", "gelu_v", "ga", "sig_a"], [f"softmax{c}" for c in n] + ["gb", "sig_b"],
              ["spatial"], [f"pv{c}" for c in n], ["gm", "br_b"], ["br_a"],
              ["merged", "mix"], ["final"]]
    assert sorted(sum(phases, [])) == sorted(by_name)
    return [[by_name[name] for name in phase] for phase in phases]


def _mixer_kernel(*refs, tile, sub):
    kt_ref, vT_ref = refs[-2:]

    @pl.when(pl.program_id(1) == 0)
    def _():
        kt_ref[...] = jnp.zeros_like(kt_ref)
        vT_ref[...] = jnp.zeros_like(vT_ref)

    n_sub = tile // sub
    states = [dict() for _ in range(n_sub)]
    consts = _attention_consts(refs[3])
    _emit_pipelined([_mixer_items(t, t * sub, sub, tile, states, consts, *refs)
                     for t in range(n_sub)], MIXER_STAGGER)


def _ffn_rows(rows, h_ref, gpre_ref, wg_ref, wu_ref, wd_ref, gpost_ref, o_ref):
    h = h_ref[rows, :]
    hn = _rms_norm(h, gpre_ref[...]).astype(_BF16)
    yield
    gate = _dot(hn, wg_ref[...])
    yield
    up = _dot(hn, wu_ref[...])
    yield
    act = (gate * _sigmoid(gate) * up).astype(_BF16)
    ff = _dot(act, wd_ref[...])
    yield
    o_ref[rows, :] = h + _rms_norm(ff, gpost_ref[...])


def _ffn_kernel(*refs, subs):
    starts = np.cumsum((0,) + subs[:-1])
    _interleave(_ffn_rows(pl.ds(int(r), n), *refs) for r, n in zip(starts, subs))


def _resident(shape):
    zeros = (0,) * len(shape)
    return pl.BlockSpec(shape, lambda *_: zeros, pipeline_mode=pl.Buffered(1))


def _vmem_limit(resident_bytes, tile_rows, row_bytes):
    need = resident_bytes + tile_rows * row_bytes
    return int(min(need, V7X_VMEM_BYTES - 8 * 1024 * 1024))


def _mixer_call(x, gpre, win, sinks, lng, lnb, ws, bst, wa, wb, wo, gpost):
    b, s, d = x.shape
    tile, sub = MIXER_TILE, MIXER_SUB
    assert s % tile == 0 and tile % sub == 0 and sub % BLOCK == 0
    weights_bytes = 2 * (win.size + wa.size + wb.size + wo.size) + 4 * ws.size
    row_bytes = 4 * d * 4 + 4 * (IN_WIDTH + 4 * D_MODEL) * 3
    tok = pl.BlockSpec((None, tile, d), lambda i, j: (i, j, 0))
    return pl.pallas_call(
        functools.partial(_mixer_kernel, tile=tile, sub=sub),
        grid=(b, s // tile),
        in_specs=[
            tok,
            _resident(gpre.shape),
            _resident(win.shape),
            _resident(sinks.shape),
            _resident(lng.shape),
            _resident(lnb.shape),
            _resident(ws.shape),
            _resident(bst.shape),
            _resident(wa.shape),
            _resident(wb.shape),
            _resident(wo.shape),
            _resident(gpost.shape),
        ],
        out_specs=tok,
        out_shape=jax.ShapeDtypeStruct(x.shape, x.dtype),
        scratch_shapes=[pltpu.VMEM((2, BLOCK, 2 * V7X_LANES), _BF16),
                        pltpu.VMEM((2, KV_WIDTH, BLOCK), _BF16)],
        compiler_params=pltpu.CompilerParams(
            dimension_semantics=("arbitrary", "arbitrary"),
            vmem_limit_bytes=_vmem_limit(weights_bytes, tile, row_bytes)),
        name="mixer",
    )(x, gpre, win, sinks, lng, lnb, ws, bst, wa, wb, wo, gpost)


def _ffn_call(h2d, gpre, wg, wu, wd, gpost):
    m, d = h2d.shape
    tile = sum(FFN_SUBS)
    assert m % tile == 0
    weights_bytes = 2 * (wg.size + wu.size + wd.size)
    row_bytes = 4 * d * 4 + 4 * D_FF * 4
    row = pl.BlockSpec((tile, d), lambda i: (i, 0))
    return pl.pallas_call(
        functools.partial(_ffn_kernel, subs=FFN_SUBS),
        grid=(m // tile,),
        in_specs=[row, _resident(gpre.shape), _resident(wg.shape), _resident(wu.shape),
                  _resident(wd.shape), _resident(gpost.shape)],
        out_specs=row,
        out_shape=jax.ShapeDtypeStruct(h2d.shape, h2d.dtype),
        compiler_params=pltpu.CompilerParams(
            dimension_semantics=("arbitrary",),
            vmem_limit_bytes=_vmem_limit(weights_bytes, tile, row_bytes)),
        name="ffn",
    )(h2d, gpre, wg, wu, wd, gpost)


def kernel(x, norm_mix_pre, w_in, attn_sinks, gmlp_ln_g, gmlp_ln_b, gmlp_w_s, gmlp_b_s,
           w_attn_branch, w_gmlp_branch, w_out, norm_mix_post, norm_ffn_pre,
           w_ffn_gate, w_ffn_up, w_ffn_down, norm_ffn_post):
    b, s, d = x.shape
    depth = w_in.shape[0]
    h = x
    for l in range(depth):
        h = _mixer_call(
            h, norm_mix_pre[l][None, :], w_in[l].astype(_BF16),
            jnp.repeat(attn_sinks[l].reshape(N_KV_HEADS, Q_PER_KV), BLOCK, axis=1),
            gmlp_ln_g[l][None, :], gmlp_ln_b[l][None, :], gmlp_w_s[l],
            gmlp_b_s[l].T, w_attn_branch[l].astype(_BF16),
            w_gmlp_branch[l].astype(_BF16), w_out[l].astype(_BF16),
            norm_mix_post[l][None, :])
        h = _ffn_call(
            h.reshape(b * s, d), norm_ffn_pre[l][None, :], w_ffn_gate[l].astype(_BF16),
            w_ffn_up[l].astype(_BF16), w_ffn_down[l].astype(_BF16),
            norm_ffn_post[l][None, :]).reshape(b, s, d)
    return h
```

```python
import functools

import jax
import jax.numpy as jnp
import numpy as np
from jax import lax
from jax.experimental import pallas as pl
from jax.experimental.pallas import tpu as pltpu

D_MODEL = 1024
EPS = 1e-6
LN_EPS = 1e-5
N_Q_HEADS = 8
N_KV_HEADS = 2
HEAD_DIM = 64
Q_PER_KV = N_Q_HEADS // N_KV_HEADS
ATTN_WIDTH = N_Q_HEADS * HEAD_DIM
KV_WIDTH = N_KV_HEADS * HEAD_DIM
BLOCK = 128
GMLP_GROUPS = 4
GMLP_WIDTH = 512
D_FF = 2816
Q_OFF = 0
K_OFF = ATTN_WIDTH
V_OFF = K_OFF + KV_WIDTH
Z_OFF = V_OFF + KV_WIDTH
GA_OFF = Z_OFF + 2 * GMLP_WIDTH
GB_OFF = GA_OFF + D_MODEL
IN_WIDTH = GB_OFF + D_MODEL

V7X_LANES = 128
BF16_SUBLANES = 16
V7X_VMEM_BYTES = 64 * 1024 * 1024
MIXER_TILE = 1024
MIXER_SUB = 256
CAST_PHASE = 1
MIXER_STAGGER = 0
FFN_SUBS = (256, 256, 256, 256)

_BF16 = jnp.bfloat16
_F32 = jnp.float32
_SQRT_HALF = np.float32(np.sqrt(0.5))
_MASKED = -1e30


def _dot(a, b):
    return jnp.dot(a, b, preferred_element_type=_F32)


def _rms_norm(x, gain):
    y = x * lax.rsqrt(jnp.mean(x * x, axis=-1, keepdims=True) + EPS)
    return y * gain


def _gelu(x):
    return 0.5 * x * (1.0 + lax.erf(x * _SQRT_HALF))


def _sigmoid(x):
    return 1.0 / (1.0 + jnp.exp(-x))


def _interleave(generators):
    live = list(generators)
    while live:
        for gen in list(live):
            try:
                next(gen)
            except StopIteration:
                live.remove(gen)


def _emit_pipelined(phases_per_sub, stagger):
    slots = sorted((k + t * stagger, t, k)
                   for t, phases in enumerate(phases_per_sub) for k in range(len(phases)))
    for _, t, k in slots:
        for item in phases_per_sub[t][k]:
            item()


def _attention_consts(sinks_ref):
    ki = lax.broadcasted_iota(jnp.int32, (2 * BLOCK, BLOCK), 0)
    qi = lax.broadcasted_iota(jnp.int32, (2 * BLOCK, BLOCK), 1)
    rel = BLOCK + qi - ki
    band = (rel >= 0) & (rel < BLOCK)
    first = band & ((ki >= BLOCK) | (pl.program_id(1) > 0))
    relf = rel.astype(_F32)
    consts = {"valid": jnp.concatenate([band] * Q_PER_KV, axis=1),
              "valid_first": jnp.concatenate([first] * Q_PER_KV, axis=1)}
    for h in range(N_KV_HEADS):
        heads = range(h * Q_PER_KV, (h + 1) * Q_PER_KV)
        consts["alibi", h] = jnp.concatenate(
            [relf * -(2.0 ** (-8.0 * (hq + 1) / N_Q_HEADS)) for hq in heads], axis=1)
        consts["sink", h] = sinks_ref[h:h + 1, :]
    return consts


def _mixer_items(t, r0, sub, tile, states, consts, x_ref, gpre_ref, win_ref, sinks_ref,
                 lng_ref, lnb_ref, ws_ref, bst_ref, wa_ref, wb_ref, wo_ref, gpost_ref,
                 h_ref, kt_ref, vT_ref):
    nb = sub // BLOCK
    st = states[t]
    chains = [(i, h) for i in range(nb) for h in range(N_KV_HEADS)]
    by_name = {}

    def add(name):
        def register(fn):
            by_name[name] = fn
            return fn
        return register

    @add("xn")
    def _():
        st["x"] = x_ref[pl.ds(r0, sub), :]
        st["xn"] = _rms_norm(st["x"], gpre_ref[...]).astype(_BF16)

    @add("qkv")
    def _():
        st["qkv"] = _dot(st["xn"], win_ref[:, Q_OFF:Z_OFF])

    @add("prep")
    def _():
        qkv = st["qkv"]
        q = (qkv[:, Q_OFF:K_OFF] * (HEAD_DIM ** -0.5)).astype(_BF16)
        lo = lax.broadcasted_iota(jnp.int32, (sub, V7X_LANES), 1) < HEAD_DIM
        kf = qkv[:, K_OFF:V_OFF]
        kr = pltpu.roll(kf, HEAD_DIM, 1)
        kt = jnp.concatenate(
            [jnp.where(lo, kf, kr), jnp.where(lo, kr, kf)], axis=1).astype(_BF16)
        vT = qkv[:, V_OFF:Z_OFF].T.astype(_BF16)
        slot = lax.rem(pl.program_id(1), 2)
        if t == 0:
            kt_prev, vT_prev = kt_ref[slot], vT_ref[slot]
        else:
            kt_prev, vT_prev = states[t - 1]["kt_last"], states[t - 1]["vT_last"]
        st["kt_last"], st["vT_last"] = kt[sub - BLOCK:, :], vT[:, sub - BLOCK:]
        if r0 + sub == tile:
            kt_ref[1 - slot] = st["kt_last"]
            vT_ref[1 - slot] = st["vT_last"]
        lo_b = lax.broadcasted_iota(jnp.int32, (BLOCK, V7X_LANES), 1) < HEAD_DIM
        for i, h in chains:
            rows = slice(i * BLOCK, (i + 1) * BLOCK)
            prev = slice((i - 1) * BLOCK, i * BLOCK)
            cols = slice(h * 2 * HEAD_DIM, (h + 1) * 2 * HEAD_DIM)
            st["kt2", i, h] = jnp.concatenate(
                [kt_prev[:, cols] if i == 0 else kt[prev, cols], kt[rows, cols]], axis=0)
            hd = slice(h * HEAD_DIM, (h + 1) * HEAD_DIM)
            st["vT2", i, h] = jnp.concatenate(
                [vT_prev[hd, :] if i == 0 else vT[hd, prev], vT[hd, rows]], axis=1)
            qh = q[rows, h * Q_PER_KV * HEAD_DIM:(h + 1) * Q_PER_KV * HEAD_DIM]
            q_lo, q_hi = qh[:, :V7X_LANES], qh[:, V7X_LANES:]
            zq = jnp.zeros_like(q_lo)
            st["q4", i, h] = jnp.concatenate(
                [jnp.where(lo_b, q_lo, zq), jnp.where(lo_b, zq, q_lo),
                 jnp.where(lo_b, q_hi, zq), jnp.where(lo_b, zq, q_hi)], axis=0)

    @add("zg")
    def _():
        st["zg"] = _dot(st["xn"], win_ref[:, Z_OFF:GA_OFF])

    @add("gelu_u")
    def _():
        st["u"] = _gelu(st["zg"][:, :GMLP_WIDTH])

    @add("gelu_v")
    def _():
        v = _gelu(st["zg"][:, GMLP_WIDTH:])
        mu = jnp.mean(v, axis=-1, keepdims=True)
        vc = v - mu
        var = jnp.mean(vc * vc, axis=-1, keepdims=True)
        st["vn"] = (vc * lax.rsqrt(var + LN_EPS) * lng_ref[...] + lnb_ref[...]).astype(_BF16)

    for c, (i, h) in enumerate(chains):
        @add(f"qk{c}")
        def _(i=i, h=h):
            st["s", i, h] = lax.dot_general(
                st["kt2", i, h], st["q4", i, h], (((1,), (1,)), ((), ())),
                preferred_element_type=_F32)

    for c, (i, h) in enumerate(chains):
        @add(f"softmax{c}")
        def _(i=i, h=h):
            valid = consts["valid_first" if r0 == 0 and i == 0 else "valid"]
            sink = consts["sink", h]
            logits = jnp.where(valid, st["s", i, h] + consts["alibi", h], _MASKED)
            m = jnp.maximum(jnp.max(logits, axis=0, keepdims=True), sink)
            p = jnp.exp(logits - m)
            denom = jnp.sum(p, axis=0, keepdims=True) + jnp.exp(sink - m)
            st["p", i, h] = p.astype(_BF16)
            st["rdenom", i, h] = 1.0 / denom

    @add("ga")
    def _():
        st["ga"] = _dot(st["xn"], win_ref[:, GA_OFF:GB_OFF])

    @add("sig_a")
    def _():
        st["ga"] = _sigmoid(st["ga"])

    @add("gb")
    def _():
        st["gb"] = _dot(st["xn"], win_ref[:, GB_OFF:IN_WIDTH])

    @add("sig_b")
    def _():
        st["gb"] = _sigmoid(st["gb"])

    @add("spatial")
    def _():
        ti = lax.broadcasted_iota(jnp.int32, (BLOCK, BLOCK), 0)
        si = lax.broadcasted_iota(jnp.int32, (BLOCK, BLOCK), 1)
        causal = si <= ti
        vn = st["vn"]
        f_cols = []
        for g in range(GMLP_GROUPS):
            cols = slice(g * BLOCK, (g + 1) * BLOCK)
            wg = jnp.where(causal, ws_ref[g], 0.0).astype(_BF16)
            rhs = jnp.concatenate(
                [vn[i * BLOCK:(i + 1) * BLOCK, cols] for i in range(nb)], axis=1)
            f_cols.append(_dot(wg, rhs) + bst_ref[:, g:g + 1])
        st["f_cols"] = f_cols

    @add("gm")
    def _():
        f = jnp.concatenate(
            [jnp.concatenate([fc[:, i * BLOCK:(i + 1) * BLOCK] for fc in st["f_cols"]], axis=1)
             for i in range(nb)], axis=0)
        st["gm"] = (st["u"] * f).astype(_BF16)

    for c, (i, h) in enumerate(chains):
        @add(f"pv{c}")
        def _(i=i, h=h):
            st["o", i, h] = (_dot(st["vT2", i, h], st["p", i, h])
                             * st["rdenom", i, h])

    @add("br_b")
    def _():
        st["br_b"] = _dot(st["gm"], wb_ref[...])

    @add("br_a")
    def _():
        attn_t = jnp.concatenate(
            [jnp.concatenate(
                [st["o", i, h][:, g * BLOCK:(g + 1) * BLOCK]
                 for h in range(N_KV_HEADS) for g in range(Q_PER_KV)], axis=0)
             for i in range(nb)], axis=1).astype(_BF16)
        st["br_a"] = lax.dot_general(attn_t, wa_ref[...], (((0,), (0,)), ((), ())),
                                     preferred_element_type=_F32)

    @add("merged")
    def _():
        st["merged"] = (st["ga"] * st["br_a"] + st["gb"] * st["br_b"]).astype(_BF16)

    @add("mix")
    def _():
        st["mix"] = _dot(st["merged"], wo_ref[...])

    @add("final")
    def _():
        h_ref[pl.ds(r0, sub), :] = st["x"] + _rms_norm(st["mix"], gpost_ref[...])

    n = range(len(chains))
    phases = [["xn"], ["qkv"], ["zg", "prep"], [f"qk{c}" for c in n],
              ["gelu_u", "gelu_v", "ga", "sig_a"], [f"softmax{c}" for c in n] + ["gb", "sig_b"],
              ["spatial"], [f"pv{c}" for c in n], ["gm", "br_b"], ["br_a"],
              ["merged", "mix"], ["final"]]
    assert sorted(sum(phases, [])) == sorted(by_name)
    return [[by_name[name] for name in phase] for phase in phases]


def _mixer_kernel(*refs, tile, sub, n_cast):
    n_in = len(refs) - 2 * n_cast - 3
    mixer_refs = refs[:n_in] + (refs[n_in + n_cast],) + refs[-2:]
    kt_ref, vT_ref = refs[-2:]


    @pl.when(pl.program_id(1) == 0)
    def _():
        kt_ref[...] = jnp.zeros_like(kt_ref)
        vT_ref[...] = jnp.zeros_like(vT_ref)

    n_sub = tile // sub
    states = [dict() for _ in range(n_sub)]
    consts = _attention_consts(mixer_refs[3])
    phases_per_sub = [_mixer_items(t, t * sub, sub, tile, states, consts, *mixer_refs)
                      for t in range(n_sub)]

    def cast_weights():
        for src, dst in zip(refs[n_in:n_in + n_cast], refs[n_in + n_cast + 1:-2]):
            dst[...] = src[...].astype(_BF16)
    phases_per_sub[-1][CAST_PHASE].append(cast_weights)
    _emit_pipelined(phases_per_sub, MIXER_STAGGER)


def _ffn_rows(rows, h_ref, gpre_ref, wg_ref, wu_ref, wd_ref, gpost_ref, o_ref):
    h = h_ref[rows, :]
    hn = _rms_norm(h, gpre_ref[...]).astype(_BF16)
    yield
    gate = _dot(hn, wg_ref[...])
    yield
    up = _dot(hn, wu_ref[...])
    yield
    act = (gate * _sigmoid(gate) * up).astype(_BF16)
    ff = _dot(act, wd_ref[...])
    yield
    o_ref[rows, :] = h + _rms_norm(ff, gpost_ref[...])


def _ffn_kernel(*refs, subs):
    starts = np.cumsum((0,) + subs[:-1])
    _interleave(_ffn_rows(pl.ds(int(r), n), *refs) for r, n in zip(starts, subs))


def _resident(shape):
    zeros = (0,) * len(shape)
    return pl.BlockSpec(shape, lambda *_: zeros, pipeline_mode=pl.Buffered(1))


def _vmem_limit(resident_bytes, tile_rows, row_bytes):
    need = resident_bytes + tile_rows * row_bytes
    return int(min(need, V7X_VMEM_BYTES - 8 * 1024 * 1024))


def _cast_spec(shape, n_steps, steps_per_seq):
    rows, cols = shape
    for n_col in range(1, n_steps + 1):
        n_row = n_steps // n_col
        if (n_row * n_col == n_steps and rows % n_row == 0 and cols % n_col == 0
                and (rows // n_row) % BF16_SUBLANES == 0 and (cols // n_col) % V7X_LANES == 0):
            def index_map(i, j, n_col=n_col):
                step = i * steps_per_seq + j
                return step // n_col, step % n_col
            return pl.BlockSpec((rows // n_row, cols // n_col), index_map)
    raise ValueError(f"no aligned {n_steps}-block split of {shape}")


def _mixer_call(x, gpre, win, sinks, lng, lnb, ws, bst, wa, wb, wo, gpost, cast_ws):
    b, s, d = x.shape
    tile, sub = MIXER_TILE, MIXER_SUB
    assert s % tile == 0 and tile % sub == 0 and sub % BLOCK == 0
    steps_per_seq = s // tile
    cast_specs = [_cast_spec(w.shape, b * steps_per_seq, steps_per_seq) for w in cast_ws]
    weights_bytes = 2 * (win.size + wa.size + wb.size + wo.size) + 4 * ws.size
    row_bytes = 4 * d * 4 + 4 * (IN_WIDTH + 4 * D_MODEL) * 3
    tok = pl.BlockSpec((None, tile, d), lambda i, j: (i, j, 0))
    return pl.pallas_call(
        functools.partial(_mixer_kernel, tile=tile, sub=sub, n_cast=len(cast_ws)),
        grid=(b, s // tile),
        in_specs=[
            tok,
            _resident(gpre.shape),
            _resident(win.shape),
            _resident(sinks.shape),
            _resident(lng.shape),
            _resident(lnb.shape),
            _resident(ws.shape),
            _resident(bst.shape),
            _resident(wa.shape),
            _resident(wb.shape),
            _resident(wo.shape),
            _resident(gpost.shape),
        ] + cast_specs,
        out_specs=[tok] + cast_specs,
        out_shape=[jax.ShapeDtypeStruct(x.shape, x.dtype)]
        + [jax.ShapeDtypeStruct(w.shape, _BF16) for w in cast_ws],
        scratch_shapes=[pltpu.VMEM((2, BLOCK, 2 * V7X_LANES), _BF16),
                        pltpu.VMEM((2, KV_WIDTH, BLOCK), _BF16)],
        compiler_params=pltpu.CompilerParams(
            dimension_semantics=("arbitrary", "arbitrary"),
            vmem_limit_bytes=_vmem_limit(weights_bytes, tile, row_bytes)),
        name="mixer",
    )(x, gpre, win, sinks, lng, lnb, ws, bst, wa, wb, wo, gpost, *cast_ws)


def _ffn_call(h2d, gpre, wg, wu, wd, gpost):
    m, d = h2d.shape
    tile = sum(FFN_SUBS)
    assert m % tile == 0
    weights_bytes = 2 * (wg.size + wu.size + wd.size)
    row_bytes = 4 * d * 4 + 4 * D_FF * 4
    row = pl.BlockSpec((tile, d), lambda i: (i, 0))
    return pl.pallas_call(
        functools.partial(_ffn_kernel, subs=FFN_SUBS),
        grid=(m // tile,),
        in_specs=[row, _resident(gpre.shape), _resident(wg.shape), _resident(wu.shape),
                  _resident(wd.shape), _resident(gpost.shape)],
        out_specs=row,
        out_shape=jax.ShapeDtypeStruct(h2d.shape, h2d.dtype),
        compiler_params=pltpu.CompilerParams(
            dimension_semantics=("arbitrary",),
            vmem_limit_bytes=_vmem_limit(weights_bytes, tile, row_bytes)),
        name="ffn",
    )(h2d, gpre, wg, wu, wd, gpost)


def kernel(x, norm_mix_pre, w_in, attn_sinks, gmlp_ln_g, gmlp_ln_b, gmlp_w_s, gmlp_b_s,
           w_attn_branch, w_gmlp_branch, w_out, norm_mix_post, norm_ffn_pre,
           w_ffn_gate, w_ffn_up, w_ffn_down, norm_ffn_post):
    b, s, d = x.shape
    depth = w_in.shape[0]
    h = x
    for l in range(depth):
        h, wg, wu, wd = _mixer_call(
            h, norm_mix_pre[l][None, :], w_in[l].astype(_BF16),
            jnp.repeat(attn_sinks[l].reshape(N_KV_HEADS, Q_PER_KV), BLOCK, axis=1),
            gmlp_ln_g[l][None, :], gmlp_ln_b[l][None, :], gmlp_w_s[l],
            gmlp_b_s[l].T, w_attn_branch[l].astype(_BF16),
            w_gmlp_branch[l].astype(_BF16), w_out[l].astype(_BF16),
            norm_mix_post[l][None, :], (w_ffn_gate[l], w_ffn_up[l], w_ffn_down[l]))
        h = _ffn_call(
            h.reshape(b * s, d), norm_ffn_pre[l][None, :], wg, wu, wd,
            norm_ffn_post[l][None, :]).reshape(b, s, d)
    return h
```

```python
import functools

import jax
import jax.numpy as jnp
import numpy as np
from jax import lax
from jax.experimental import pallas as pl
from jax.experimental.pallas import tpu as pltpu

D_MODEL = 1024
EPS = 1e-6
LN_EPS = 1e-5
N_Q_HEADS = 8
N_KV_HEADS = 2
HEAD_DIM = 64
Q_PER_KV = N_Q_HEADS // N_KV_HEADS
ATTN_WIDTH = N_Q_HEADS * HEAD_DIM
KV_WIDTH = N_KV_HEADS * HEAD_DIM
BLOCK = 128
GMLP_GROUPS = 4
GMLP_WIDTH = 512
D_FF = 2816
Q_OFF = 0
K_OFF = ATTN_WIDTH
V_OFF = K_OFF + KV_WIDTH
Z_OFF = V_OFF + KV_WIDTH
GA_OFF = Z_OFF + 2 * GMLP_WIDTH
GB_OFF = GA_OFF + D_MODEL
IN_WIDTH = GB_OFF + D_MODEL

V7X_LANES = 128
BF16_SUBLANES = 16
V7X_VMEM_BYTES = 64 * 1024 * 1024
MIXER_TILE = 1024
MIXER_SUB = 256
CAST_PHASE = 1
FFN_TILE = 1024
FFN_SUB = 256

_BF16 = jnp.bfloat16
_F32 = jnp.float32
_SQRT_HALF = np.float32(np.sqrt(0.5))
_MASKED = -1e30


def _dot(a, b):
    return jnp.dot(a, b, preferred_element_type=_F32)


def _rms_norm(x, gain):
    y = x * lax.rsqrt(jnp.mean(x * x, axis=-1, keepdims=True) + EPS)
    return y * gain


def _gelu(x):
    return 0.5 * x * (1.0 + lax.erf(x * _SQRT_HALF))


def _sigmoid(x):
    return 1.0 / (1.0 + jnp.exp(-x))


def _emit_round_robin(phases_per_sub):
    for phase_of_each in zip(*phases_per_sub):
        for phase in phase_of_each:
            for item in phase:
                item()


def _attention_consts(sinks_ref):
    ki = lax.broadcasted_iota(jnp.int32, (2 * BLOCK, BLOCK), 0)
    qi = lax.broadcasted_iota(jnp.int32, (2 * BLOCK, BLOCK), 1)
    rel = BLOCK + qi - ki
    band = (rel >= 0) & (rel < BLOCK)
    first = band & ((ki >= BLOCK) | (pl.program_id(1) > 0))
    relf = rel.astype(_F32)
    consts = {"valid": jnp.concatenate([band] * Q_PER_KV, axis=1),
              "valid_first": jnp.concatenate([first] * Q_PER_KV, axis=1)}
    for h in range(N_KV_HEADS):
        heads = range(h * Q_PER_KV, (h + 1) * Q_PER_KV)
        consts["alibi", h] = jnp.concatenate(
            [relf * -(2.0 ** (-8.0 * (hq + 1) / N_Q_HEADS)) for hq in heads], axis=1)
        consts["sink", h] = sinks_ref[h:h + 1, :]
    return consts


def _mixer_items(t, r0, sub, tile, states, consts, x_ref, gpre_ref, win_ref, sinks_ref,
                 lng_ref, lnb_ref, ws_ref, bst_ref, wa_ref, wb_ref, wo_ref, gpost_ref,
                 h_ref, kt_ref, vT_ref):
    nb = sub // BLOCK
    st = states[t]
    chains = [(i, h) for i in range(nb) for h in range(N_KV_HEADS)]
    by_name = {}

    def add(name):
        def register(fn):
            by_name[name] = fn
            return fn
        return register

    @add("xn")
    def _():
        st["x"] = x_ref[pl.ds(r0, sub), :]
        st["xn"] = _rms_norm(st["x"], gpre_ref[...]).astype(_BF16)

    @add("qkv")
    def _():
        st["qkv"] = _dot(st["xn"], win_ref[:, Q_OFF:Z_OFF])

    @add("prep")
    def _():
        qkv = st["qkv"]
        q = (qkv[:, Q_OFF:K_OFF] * (HEAD_DIM ** -0.5)).astype(_BF16)
        lo = lax.broadcasted_iota(jnp.int32, (sub, V7X_LANES), 1) < HEAD_DIM
        kf = qkv[:, K_OFF:V_OFF]
        kr = pltpu.roll(kf, HEAD_DIM, 1)
        kt = jnp.concatenate(
            [jnp.where(lo, kf, kr), jnp.where(lo, kr, kf)], axis=1).astype(_BF16)
        vT = qkv[:, V_OFF:Z_OFF].T.astype(_BF16)
        slot = lax.rem(pl.program_id(1), 2)
        if t == 0:
            kt_prev, vT_prev = kt_ref[slot], vT_ref[slot]
        else:
            kt_prev, vT_prev = states[t - 1]["kt_last"], states[t - 1]["vT_last"]
        st["kt_last"], st["vT_last"] = kt[sub - BLOCK:, :], vT[:, sub - BLOCK:]
        if r0 + sub == tile:
            kt_ref[1 - slot] = st["kt_last"]
            vT_ref[1 - slot] = st["vT_last"]
        lo_b = lax.broadcasted_iota(jnp.int32, (BLOCK, V7X_LANES), 1) < HEAD_DIM
        for i, h in chains:
            rows = slice(i * BLOCK, (i + 1) * BLOCK)
            prev = slice((i - 1) * BLOCK, i * BLOCK)
            cols = slice(h * 2 * HEAD_DIM, (h + 1) * 2 * HEAD_DIM)
            st["kt2", i, h] = jnp.concatenate(
                [kt_prev[:, cols] if i == 0 else kt[prev, cols], kt[rows, cols]], axis=0)
            hd = slice(h * HEAD_DIM, (h + 1) * HEAD_DIM)
            st["vT2", i, h] = jnp.concatenate(
                [vT_prev[hd, :] if i == 0 else vT[hd, prev], vT[hd, rows]], axis=1)
            qh = q[rows, h * Q_PER_KV * HEAD_DIM:(h + 1) * Q_PER_KV * HEAD_DIM]
            q_lo, q_hi = qh[:, :V7X_LANES], qh[:, V7X_LANES:]
            zq = jnp.zeros_like(q_lo)
            st["q4", i, h] = jnp.concatenate(
                [jnp.where(lo_b, q_lo, zq), jnp.where(lo_b, zq, q_lo),
                 jnp.where(lo_b, q_hi, zq), jnp.where(lo_b, zq, q_hi)], axis=0)

    @add("zg")
    def _():
        st["zg"] = _dot(st["xn"], win_ref[:, Z_OFF:GA_OFF])

    @add("gelu_u")
    def _():
        st["u"] = _gelu(st["zg"][:, :GMLP_WIDTH])

    @add("gelu_v")
    def _():
        v = _gelu(st["zg"][:, GMLP_WIDTH:])
        mu = jnp.mean(v, axis=-1, keepdims=True)
        vc = v - mu
        var = jnp.mean(vc * vc, axis=-1, keepdims=True)
        st["vn"] = (vc * lax.rsqrt(var + LN_EPS) * lng_ref[...] + lnb_ref[...]).astype(_BF16)

    for c, (i, h) in enumerate(chains):
        @add(f"qk{c}")
        def _(i=i, h=h):
            st["s", i, h] = lax.dot_general(
                st["kt2", i, h], st["q4", i, h], (((1,), (1,)), ((), ())),
                preferred_element_type=_F32)

    for c, (i, h) in enumerate(chains):
        @add(f"softmax{c}")
        def _(i=i, h=h):
            valid = consts["valid_first" if r0 == 0 and i == 0 else "valid"]
            sink = consts["sink", h]
            logits = jnp.where(valid, st["s", i, h] + consts["alibi", h], _MASKED)
            m = jnp.maximum(jnp.max(logits, axis=0, keepdims=True), sink)
            p = jnp.exp(logits - m)
            denom = jnp.sum(p, axis=0, keepdims=True) + jnp.exp(sink - m)
            st["p", i, h] = p.astype(_BF16)
            st["rdenom", i, h] = 1.0 / denom

    @add("ga")
    def _():
        st["ga"] = _dot(st["xn"], win_ref[:, GA_OFF:GB_OFF])

    @add("sig_a")
    def _():
        st["ga"] = _sigmoid(st["ga"])

    @add("gb")
    def _():
        st["gb"] = _dot(st["xn"], win_ref[:, GB_OFF:IN_WIDTH])

    @add("sig_b")
    def _():
        st["gb"] = _sigmoid(st["gb"])

    @add("spatial")
    def _():
        ti = lax.broadcasted_iota(jnp.int32, (BLOCK, BLOCK), 0)
        si = lax.broadcasted_iota(jnp.int32, (BLOCK, BLOCK), 1)
        causal = si <= ti
        vn = st["vn"]
        f_cols = []
        for g in range(GMLP_GROUPS):
            cols = slice(g * BLOCK, (g + 1) * BLOCK)
            wg = jnp.where(causal, ws_ref[g], 0.0).astype(_BF16)
            rhs = jnp.concatenate(
                [vn[i * BLOCK:(i + 1) * BLOCK, cols] for i in range(nb)], axis=1)
            f_cols.append(_dot(wg, rhs) + bst_ref[:, g:g + 1])
        st["f_cols"] = f_cols

    @add("gm")
    def _():
        f = jnp.concatenate(
            [jnp.concatenate([fc[:, i * BLOCK:(i + 1) * BLOCK] for fc in st["f_cols"]], axis=1)
             for i in range(nb)], axis=0)
        st["gm"] = (st["u"] * f).astype(_BF16)

    for c, (i, h) in enumerate(chains):
        @add(f"pv{c}")
        def _(i=i, h=h):
            st["o", i, h] = (_dot(st["vT2", i, h], st["p", i, h])
                             * st["rdenom", i, h])

    @add("br_b")
    def _():
        st["br_b"] = _dot(st["gm"], wb_ref[...])

    @add("br_a")
    def _():
        attn_t = jnp.concatenate(
            [jnp.concatenate(
                [st["o", i, h][:, g * BLOCK:(g + 1) * BLOCK]
                 for h in range(N_KV_HEADS) for g in range(Q_PER_KV)], axis=0)
             for i in range(nb)], axis=1).astype(_BF16)
        st["br_a"] = lax.dot_general(attn_t, wa_ref[...], (((0,), (0,)), ((), ())),
                                     preferred_element_type=_F32)

    @add("merged")
    def _():
        st["merged"] = (st["ga"] * st["br_a"] + st["gb"] * st["br_b"]).astype(_BF16)

    @add("mix")
    def _():
        st["mix"] = _dot(st["merged"], wo_ref[...])

    @add("final")
    def _():
        h_ref[pl.ds(r0, sub), :] = st["x"] + _rms_norm(st["mix"], gpost_ref[...])

    n = range(len(chains))
    phases = [["xn"], ["qkv"], ["zg", "prep"], [f"qk{c}" for c in n],
              ["gelu_u", "gelu_v", "ga", "sig_a"], [f"softmax{c}" for c in n] + ["gb", "sig_b"],
              ["spatial"], [f"pv{c}" for c in n], ["gm", "br_b"], ["br_a"],
              ["merged", "mix"], ["final"]]
    assert sorted(sum(phases, [])) == sorted(by_name)
    return [[by_name[name] for name in phase] for phase in phases]


def _mixer_kernel(*refs, tile, sub, n_cast):
    n_in = len(refs) - 2 * n_cast - 3
    mixer_refs = refs[:n_in] + (refs[n_in + n_cast],) + refs[-2:]
    kt_ref, vT_ref = refs[-2:]

    @pl.when(pl.program_id(1) == 0)
    def _():
        kt_ref[...] = jnp.zeros_like(kt_ref)
        vT_ref[...] = jnp.zeros_like(vT_ref)

    n_sub = tile // sub
    states = [dict() for _ in range(n_sub)]
    consts = _attention_consts(mixer_refs[3])
    phases_per_sub = [_mixer_items(t, t * sub, sub, tile, states, consts, *mixer_refs)
                      for t in range(n_sub)]

    def cast_weights():
        for src, dst in zip(refs[n_in:n_in + n_cast], refs[n_in + n_cast + 1:-2]):
            dst[...] = src[...].astype(_BF16)
    phases_per_sub[-1][CAST_PHASE].append(cast_weights)
    _emit_round_robin(phases_per_sub)


def _ffn_phases(r0, sub, h_ref, gpre_ref, wg_ref, wu_ref, wd_ref, gpost_ref, o_ref):
    st = {}
    rows = pl.ds(r0, sub)

    def norm():
        st["h"] = h_ref[rows, :]
        st["hn"] = _rms_norm(st["h"], gpre_ref[...]).astype(_BF16)

    def gate():
        st["gate"] = _dot(st["hn"], wg_ref[...])

    def up():
        st["up"] = _dot(st["hn"], wu_ref[...])

    def down():
        act = (st["gate"] * _sigmoid(st["gate"]) * st["up"]).astype(_BF16)
        st["ff"] = _dot(act, wd_ref[...])

    def final():
        o_ref[rows, :] = st["h"] + _rms_norm(st["ff"], gpost_ref[...])

    return [[norm], [gate], [up], [down], [final]]


def _ffn_kernel(*refs, tile, sub):
    _emit_round_robin([_ffn_phases(r0, sub, *refs) for r0 in range(0, tile, sub)])


def _resident(shape):
    zeros = (0,) * len(shape)
    return pl.BlockSpec(shape, lambda *_: zeros, pipeline_mode=pl.Buffered(1))


def _vmem_limit(resident_bytes, tile_rows, row_bytes):
    need = resident_bytes + tile_rows * row_bytes
    return int(min(need, V7X_VMEM_BYTES - 8 * 1024 * 1024))


def _cast_spec(shape, n_steps, steps_per_seq):
    rows, cols = shape
    for n_col in range(1, n_steps + 1):
        n_row = n_steps // n_col
        if (n_row * n_col == n_steps and rows % n_row == 0 and cols % n_col == 0
                and (rows // n_row) % BF16_SUBLANES == 0 and (cols // n_col) % V7X_LANES == 0):
            def index_map(i, j, n_col=n_col):
                step = i * steps_per_seq + j
                return step // n_col, step % n_col
            return pl.BlockSpec((rows // n_row, cols // n_col), index_map)
    raise ValueError(f"no aligned {n_steps}-block split of {shape}")


def _mixer_call(x, gpre, win, sinks, lng, lnb, ws, bst, wa, wb, wo, gpost, cast_ws):
    b, s, d = x.shape
    tile, sub = MIXER_TILE, MIXER_SUB
    assert s % tile == 0 and tile % sub == 0 and sub % BLOCK == 0
    steps_per_seq = s // tile
    cast_specs = [_cast_spec(w.shape, b * steps_per_seq, steps_per_seq) for w in cast_ws]
    weights_bytes = 2 * (win.size + wa.size + wb.size + wo.size) + 4 * ws.size
    row_bytes = 4 * d * 4 + 4 * (IN_WIDTH + 4 * D_MODEL) * 3
    tok = pl.BlockSpec((None, tile, d), lambda i, j: (i, j, 0))
    return pl.pallas_call(
        functools.partial(_mixer_kernel, tile=tile, sub=sub, n_cast=len(cast_ws)),
        grid=(b, s // tile),
        in_specs=[
            tok,
            _resident(gpre.shape),
            _resident(win.shape),
            _resident(sinks.shape),
            _resident(lng.shape),
            _resident(lnb.shape),
            _resident(ws.shape),
            _resident(bst.shape),
            _resident(wa.shape),
            _resident(wb.shape),
            _resident(wo.shape),
            _resident(gpost.shape),
        ] + cast_specs,
        out_specs=[tok] + cast_specs,
        out_shape=[jax.ShapeDtypeStruct(x.shape, x.dtype)]
        + [jax.ShapeDtypeStruct(w.shape, _BF16) for w in cast_ws],
        scratch_shapes=[pltpu.VMEM((2, BLOCK, 2 * V7X_LANES), _BF16),
                        pltpu.VMEM((2, KV_WIDTH, BLOCK), _BF16)],
        compiler_params=pltpu.CompilerParams(
            dimension_semantics=("arbitrary", "arbitrary"),
            vmem_limit_bytes=_vmem_limit(weights_bytes, tile, row_bytes)),
        name="mixer",
    )(x, gpre, win, sinks, lng, lnb, ws, bst, wa, wb, wo, gpost, *cast_ws)


def _ffn_call(h2d, gpre, wg, wu, wd, gpost):
    m, d = h2d.shape
    tile, sub = FFN_TILE, FFN_SUB
    assert m % tile == 0 and tile % sub == 0
    weights_bytes = 2 * (wg.size + wu.size + wd.size)
    row_bytes = 4 * d * 4 + 4 * D_FF * 4
    row = pl.BlockSpec((tile, d), lambda i: (i, 0))
    return pl.pallas_call(
        functools.partial(_ffn_kernel, tile=tile, sub=sub),
        grid=(m // tile,),
        in_specs=[row, _resident(gpre.shape), _resident(wg.shape), _resident(wu.shape),
                  _resident(wd.shape), _resident(gpost.shape)],
        out_specs=row,
        out_shape=jax.ShapeDtypeStruct(h2d.shape, h2d.dtype),
        compiler_params=pltpu.CompilerParams(
            dimension_semantics=("arbitrary",),
            vmem_limit_bytes=_vmem_limit(weights_bytes, tile, row_bytes)),
        name="ffn",
    )(h2d, gpre, wg, wu, wd, gpost)


def kernel(x, norm_mix_pre, w_in, attn_sinks, gmlp_ln_g, gmlp_ln_b, gmlp_w_s, gmlp_b_s,
           w_attn_branch, w_gmlp_branch, w_out, norm_mix_post, norm_ffn_pre,
           w_ffn_gate, w_ffn_up, w_ffn_down, norm_ffn_post):
    b, s, d = x.shape
    depth = w_in.shape[0]
    h = x
    for l in range(depth):
        h, wg, wu, wd = _mixer_call(
            h, norm_mix_pre[l][None, :], w_in[l].astype(_BF16),
            jnp.repeat(attn_sinks[l].reshape(N_KV_HEADS, Q_PER_KV), BLOCK, axis=1),
            gmlp_ln_g[l][None, :], gmlp_ln_b[l][None, :], gmlp_w_s[l],
            gmlp_b_s[l].T, w_attn_branch[l].astype(_BF16),
            w_gmlp_branch[l].astype(_BF16), w_out[l].astype(_BF16),
            norm_mix_post[l][None, :], (w_ffn_gate[l], w_ffn_up[l], w_ffn_down[l]))
        h = _ffn_call(
            h.reshape(b * s, d), norm_ffn_pre[l][None, :], wg, wu, wd,
            norm_ffn_post[l][None, :]).reshape(b, s, d)
    return h
```

```python
import functools

import jax
import jax.numpy as jnp
import numpy as np
from jax import lax
from jax.experimental import pallas as pl
from jax.experimental.pallas import tpu as pltpu

D_MODEL = 1024
EPS = 1e-6
LN_EPS = 1e-5
N_Q_HEADS = 8
N_KV_HEADS = 2
HEAD_DIM = 64
Q_PER_KV = N_Q_HEADS // N_KV_HEADS
ATTN_WIDTH = N_Q_HEADS * HEAD_DIM
KV_WIDTH = N_KV_HEADS * HEAD_DIM
BLOCK = 128
GMLP_GROUPS = 4
GMLP_WIDTH = 512
D_FF = 2816
Q_OFF = 0
K_OFF = ATTN_WIDTH
V_OFF = K_OFF + KV_WIDTH
Z_OFF = V_OFF + KV_WIDTH
GA_OFF = Z_OFF + 2 * GMLP_WIDTH
GB_OFF = GA_OFF + D_MODEL
IN_WIDTH = GB_OFF + D_MODEL

V7X_LANES = 128
BF16_SUBLANES = 16
V7X_VMEM_BYTES = 64 * 1024 * 1024
V7X_VMEM_RESERVE_BYTES = 8 * 1024 * 1024
MIXER_TILE = 1024
MIXER_SUB = 256
CAST_PHASE = 1
FFN_TILE = 1024
FFN_SUB = 256

_BF16 = jnp.bfloat16
_F32 = jnp.float32
_SQRT_HALF = np.float32(np.sqrt(0.5))
_MASKED = -1e30


def _dot(a, b):
    return jnp.dot(a, b, preferred_element_type=_F32)


def _rms_norm(x, gain):
    y = x * lax.rsqrt(jnp.mean(x * x, axis=-1, keepdims=True) + EPS)
    return y * gain


def _gelu(x):
    return 0.5 * x * (1.0 + lax.erf(x * _SQRT_HALF))


def _sigmoid(x):
    return 1.0 / (1.0 + jnp.exp(-x))


def _emit_round_robin(phases_per_sub):
    for phase_of_each in zip(*phases_per_sub):
        for phase in phase_of_each:
            for item in phase:
                item()


def _attention_consts(sinks_ref):
    ki = lax.broadcasted_iota(jnp.int32, (2 * BLOCK, BLOCK), 0)
    qi = lax.broadcasted_iota(jnp.int32, (2 * BLOCK, BLOCK), 1)
    rel = BLOCK + qi - ki
    band = (rel >= 0) & (rel < BLOCK)
    first = band & ((ki >= BLOCK) | (pl.program_id(1) > 0))
    relf = rel.astype(_F32)
    consts = {"valid": jnp.concatenate([band] * Q_PER_KV, axis=1),
              "valid_first": jnp.concatenate([first] * Q_PER_KV, axis=1)}
    for h in range(N_KV_HEADS):
        heads = range(h * Q_PER_KV, (h + 1) * Q_PER_KV)
        consts["alibi", h] = jnp.concatenate(
            [relf * -(2.0 ** (-8.0 * (hq + 1) / N_Q_HEADS)) for hq in heads], axis=1)
        consts["sink", h] = sinks_ref[h:h + 1, :]
    return consts


def _mixer_items(t, r0, sub, tile, states, consts, x_ref, gpre_ref, win_ref, sinks_ref,
                 lng_ref, lnb_ref, ws_ref, bst_ref, wa_ref, wb_ref, wo_ref, gpost_ref,
                 h_ref, kt_ref, vT_ref):
    nb = sub // BLOCK
    st = states[t]
    chains = [(i, h) for i in range(nb) for h in range(N_KV_HEADS)]
    by_name = {}

    def add(name):
        def register(fn):
            by_name[name] = fn
            return fn
        return register

    @add("xn")
    def _():
        st["x"] = x_ref[pl.ds(r0, sub), :]
        st["xn"] = _rms_norm(st["x"], gpre_ref[...]).astype(_BF16)

    @add("qkv")
    def _():
        st["qkv"] = _dot(st["xn"], win_ref[:, Q_OFF:Z_OFF])

    @add("prep")
    def _():
        qkv = st["qkv"]
        q = (qkv[:, Q_OFF:K_OFF] * (HEAD_DIM ** -0.5)).astype(_BF16)
        lo = lax.broadcasted_iota(jnp.int32, (sub, V7X_LANES), 1) < HEAD_DIM
        kf = qkv[:, K_OFF:V_OFF]
        kr = pltpu.roll(kf, HEAD_DIM, 1)
        kt = jnp.concatenate(
            [jnp.where(lo, kf, kr), jnp.where(lo, kr, kf)], axis=1).astype(_BF16)
        vT = qkv[:, V_OFF:Z_OFF].T.astype(_BF16)
        slot = lax.rem(pl.program_id(1), 2)
        if t == 0:
            kt_prev, vT_prev = kt_ref[slot], vT_ref[slot]
        else:
            kt_prev, vT_prev = states[t - 1]["kt_last"], states[t - 1]["vT_last"]
        st["kt_last"], st["vT_last"] = kt[sub - BLOCK:, :], vT[:, sub - BLOCK:]
        if r0 + sub == tile:
            kt_ref[1 - slot] = st["kt_last"]
            vT_ref[1 - slot] = st["vT_last"]
        lo_b = lax.broadcasted_iota(jnp.int32, (BLOCK, V7X_LANES), 1) < HEAD_DIM
        for i, h in chains:
            rows = slice(i * BLOCK, (i + 1) * BLOCK)
            prev = slice((i - 1) * BLOCK, i * BLOCK)
            cols = slice(h * 2 * HEAD_DIM, (h + 1) * 2 * HEAD_DIM)
            st["kt2", i, h] = jnp.concatenate(
                [kt_prev[:, cols] if i == 0 else kt[prev, cols], kt[rows, cols]], axis=0)
            hd = slice(h * HEAD_DIM, (h + 1) * HEAD_DIM)
            st["vT2", i, h] = jnp.concatenate(
                [vT_prev[hd, :] if i == 0 else vT[hd, prev], vT[hd, rows]], axis=1)
            qh = q[rows, h * Q_PER_KV * HEAD_DIM:(h + 1) * Q_PER_KV * HEAD_DIM]
            q_lo, q_hi = qh[:, :V7X_LANES], qh[:, V7X_LANES:]
            zq = jnp.zeros_like(q_lo)
            st["q4", i, h] = jnp.concatenate(
                [jnp.where(lo_b, q_lo, zq), jnp.where(lo_b, zq, q_lo),
                 jnp.where(lo_b, q_hi, zq), jnp.where(lo_b, zq, q_hi)], axis=0)

    @add("zg")
    def _():
        st["zg"] = _dot(st["xn"], win_ref[:, Z_OFF:GA_OFF])

    @add("gelu_u")
    def _():
        st["u"] = _gelu(st["zg"][:, :GMLP_WIDTH])

    @add("gelu_v")
    def _():
        v = _gelu(st["zg"][:, GMLP_WIDTH:])
        mu = jnp.mean(v, axis=-1, keepdims=True)
        vc = v - mu
        var = jnp.mean(vc * vc, axis=-1, keepdims=True)
        st["vn"] = (vc * lax.rsqrt(var + LN_EPS) * lng_ref[...] + lnb_ref[...]).astype(_BF16)

    for c, (i, h) in enumerate(chains):
        @add(f"qk{c}")
        def _(i=i, h=h):
            st["s", i, h] = lax.dot_general(
                st["kt2", i, h], st["q4", i, h], (((1,), (1,)), ((), ())),
                preferred_element_type=_F32)

    for c, (i, h) in enumerate(chains):
        @add(f"softmax{c}")
        def _(i=i, h=h):
            valid = consts["valid_first" if r0 == 0 and i == 0 else "valid"]
            sink = consts["sink", h]
            logits = jnp.where(valid, st["s", i, h] + consts["alibi", h], _MASKED)
            m = jnp.maximum(jnp.max(logits, axis=0, keepdims=True), sink)
            p = jnp.exp(logits - m)
            denom = jnp.sum(p, axis=0, keepdims=True) + jnp.exp(sink - m)
            st["p", i, h] = p.astype(_BF16)
            st["rdenom", i, h] = 1.0 / denom

    @add("ga")
    def _():
        st["ga"] = _dot(st["xn"], win_ref[:, GA_OFF:GB_OFF])

    @add("sig_a")
    def _():
        st["ga"] = _sigmoid(st["ga"])

    @add("gb")
    def _():
        st["gb"] = _dot(st["xn"], win_ref[:, GB_OFF:IN_WIDTH])

    @add("sig_b")
    def _():
        st["gb"] = _sigmoid(st["gb"])

    @add("spatial")
    def _():
        ti = lax.broadcasted_iota(jnp.int32, (BLOCK, BLOCK), 0)
        si = lax.broadcasted_iota(jnp.int32, (BLOCK, BLOCK), 1)
        causal = si <= ti
        vn = st["vn"]
        f_cols = []
        for g in range(GMLP_GROUPS):
            cols = slice(g * BLOCK, (g + 1) * BLOCK)
            wg = jnp.where(causal, ws_ref[g], 0.0).astype(_BF16)
            rhs = jnp.concatenate(
                [vn[i * BLOCK:(i + 1) * BLOCK, cols] for i in range(nb)], axis=1)
            f_cols.append(_dot(wg, rhs) + bst_ref[:, g:g + 1])
        st["f_cols"] = f_cols

    @add("gm")
    def _():
        f = jnp.concatenate(
            [jnp.concatenate([fc[:, i * BLOCK:(i + 1) * BLOCK] for fc in st["f_cols"]], axis=1)
             for i in range(nb)], axis=0)
        st["gm"] = (st["u"] * f).astype(_BF16)

    for c, (i, h) in enumerate(chains):
        @add(f"pv{c}")
        def _(i=i, h=h):
            st["o", i, h] = (_dot(st["vT2", i, h], st["p", i, h])
                             * st["rdenom", i, h])

    @add("br_b")
    def _():
        st["br_b"] = _dot(st["gm"], wb_ref[...])

    @add("br_a")
    def _():
        attn_t = jnp.concatenate(
            [jnp.concatenate(
                [st["o", i, h][:, g * BLOCK:(g + 1) * BLOCK]
                 for h in range(N_KV_HEADS) for g in range(Q_PER_KV)], axis=0)
             for i in range(nb)], axis=1).astype(_BF16)
        st["br_a"] = lax.dot_general(attn_t, wa_ref[...], (((0,), (0,)), ((), ())),
                                     preferred_element_type=_F32)

    @add("merged")
    def _():
        st["merged"] = (st["ga"] * st["br_a"] + st["gb"] * st["br_b"]).astype(_BF16)

    @add("mix")
    def _():
        st["mix"] = _dot(st["merged"], wo_ref[...])

    @add("final")
    def _():
        h_ref[pl.ds(r0, sub), :] = st["x"] + _rms_norm(st["mix"], gpost_ref[...])

    n = range(len(chains))
    phases = [["xn"], ["qkv"], ["zg", "prep"], [f"qk{c}" for c in n],
              ["gelu_u", "gelu_v", "ga", "sig_a"], [f"softmax{c}" for c in n] + ["gb", "sig_b"],
              ["spatial"], [f"pv{c}" for c in n], ["gm", "br_b"], ["br_a"],
              ["merged", "mix"], ["final"]]
    assert sorted(sum(phases, [])) == sorted(by_name)
    return [[by_name[name] for name in phase] for phase in phases]


def _mixer_kernel(*refs, tile, sub, n_cast):
    n_in = len(refs) - 2 * n_cast - 3
    mixer_refs = refs[:n_in] + (refs[n_in + n_cast],) + refs[-2:]
    kt_ref, vT_ref = refs[-2:]

    @pl.when(pl.program_id(1) == 0)
    def _():
        kt_ref[...] = jnp.zeros_like(kt_ref)
        vT_ref[...] = jnp.zeros_like(vT_ref)

    n_sub = tile // sub
    states = [dict() for _ in range(n_sub)]
    consts = _attention_consts(mixer_refs[3])
    phases_per_sub = [_mixer_items(t, t * sub, sub, tile, states, consts, *mixer_refs)
                      for t in range(n_sub)]

    def cast_weights():
        for src, dst in zip(refs[n_in:n_in + n_cast], refs[n_in + n_cast + 1:-2]):
            dst[...] = src[...].astype(_BF16)
    phases_per_sub[-1][CAST_PHASE].append(cast_weights)
    _emit_round_robin(phases_per_sub)


def _ffn_phases(r0, sub, h_ref, gpre_ref, wg_ref, wu_ref, wd_ref, gpost_ref, o_ref):
    st = {}
    rows = pl.ds(r0, sub)

    def norm():
        st["h"] = h_ref[rows, :]
        st["hn"] = _rms_norm(st["h"], gpre_ref[...]).astype(_BF16)

    def gate():
        st["gate"] = _dot(st["hn"], wg_ref[...])

    def up():
        st["up"] = _dot(st["hn"], wu_ref[...])

    def down():
        act = (st["gate"] * _sigmoid(st["gate"]) * st["up"]).astype(_BF16)
        st["ff"] = _dot(act, wd_ref[...])

    def final():
        o_ref[rows, :] = st["h"] + _rms_norm(st["ff"], gpost_ref[...])

    return [[norm], [gate], [up], [down], [final]]


def _ffn_kernel(*refs, tile, sub):
    _emit_round_robin([_ffn_phases(r0, sub, *refs) for r0 in range(0, tile, sub)])


def _resident(shape):
    zeros = (0,) * len(shape)
    return pl.BlockSpec(shape, lambda *_: zeros, pipeline_mode=pl.Buffered(1))


def _vmem_limit(resident_bytes, block_bytes):
    budget = V7X_VMEM_BYTES - V7X_VMEM_RESERVE_BYTES
    assert resident_bytes + 6 * block_bytes <= budget, (resident_bytes, block_bytes)
    return budget


def _cast_spec(shape, n_steps, steps_per_seq):
    rows, cols = shape
    for n_col in range(1, n_steps + 1):
        n_row = n_steps // n_col
        if (n_row * n_col == n_steps and rows % n_row == 0 and cols % n_col == 0
                and (rows // n_row) % BF16_SUBLANES == 0 and (cols // n_col) % V7X_LANES == 0):
            def index_map(i, j, n_col=n_col):
                step = i * steps_per_seq + j
                return step // n_col, step % n_col
            return pl.BlockSpec((rows // n_row, cols // n_col), index_map)
    raise ValueError(f"no aligned {n_steps}-block split of {shape}")


def _mixer_call(x, gpre, win, sinks, lng, lnb, ws, bst, wa, wb, wo, gpost, cast_ws):
    b, s, d = x.shape
    tile, sub = MIXER_TILE, MIXER_SUB
    assert s % tile == 0 and tile % sub == 0 and sub % BLOCK == 0
    steps_per_seq = s // tile
    cast_specs = [_cast_spec(w.shape, b * steps_per_seq, steps_per_seq) for w in cast_ws]
    weights_bytes = 2 * (win.size + wa.size + wb.size + wo.size) + 4 * ws.size
    tok =pl.BlockSpec((None, tile, d), lambda i, j: (i, j, 0))
    return pl.pallas_call(
        functools.partial(_mixer_kernel, tile=tile, sub=sub, n_cast=len(cast_ws)),
        grid=(b, s // tile),
        in_specs=[
            tok,
            _resident(gpre.shape),
            _resident(win.shape),
            _resident(sinks.shape),
            _resident(lng.shape),
            _resident(lnb.shape),
            _resident(ws.shape),
            _resident(bst.shape),
            _resident(wa.shape),
            _resident(wb.shape),
            _resident(wo.shape),
            _resident(gpost.shape),
        ] + cast_specs,
        out_specs=[tok] + cast_specs,
        out_shape=[jax.ShapeDtypeStruct(x.shape, x.dtype)]
        + [jax.ShapeDtypeStruct(w.shape, _BF16) for w in cast_ws],
        scratch_shapes=[pltpu.VMEM((2, BLOCK, 2 * V7X_LANES), _BF16),
                        pltpu.VMEM((2, KV_WIDTH, BLOCK), _BF16)],
        compiler_params=pltpu.CompilerParams(
            dimension_semantics=("arbitrary", "arbitrary"),
            vmem_limit_bytes=_vmem_limit(weights_bytes, tile * d * x.dtype.itemsize)),
        name="mixer",
    )(x, gpre, win, sinks, lng, lnb, ws, bst, wa, wb, wo, gpost, *cast_ws)


def _ffn_call(h2d, gpre, wg, wu, wd, gpost):
    m, d = h2d.shape
    tile, sub = FFN_TILE, FFN_SUB
    assert m % tile == 0 and tile % sub == 0
    weights_bytes = 2 * (wg.size + wu.size + wd.size)
    row =pl.BlockSpec((tile, d), lambda i: (i, 0))
    return pl.pallas_call(
        functools.partial(_ffn_kernel, tile=tile, sub=sub),
        grid=(m // tile,),
        in_specs=[row, _resident(gpre.shape), _resident(wg.shape), _resident(wu.shape),
                  _resident(wd.shape), _resident(gpost.shape)],
        out_specs=row,
        out_shape=jax.ShapeDtypeStruct(h2d.shape, h2d.dtype),
        compiler_params=pltpu.CompilerParams(
            dimension_semantics=("arbitrary",),
            vmem_limit_bytes=_vmem_limit(weights_bytes, tile * d * h2d.dtype.itemsize)),
        name="ffn",
    )(h2d, gpre, wg, wu, wd, gpost)


def kernel(x, norm_mix_pre, w_in, attn_sinks, gmlp_ln_g, gmlp_ln_b, gmlp_w_s, gmlp_b_s,
           w_attn_branch, w_gmlp_branch, w_out, norm_mix_post, norm_ffn_pre,
           w_ffn_gate, w_ffn_up, w_ffn_down, norm_ffn_post):
    b, s, d = x.shape
    depth = w_in.shape[0]
    h = x
    for l in range(depth):
        h, wg, wu, wd = _mixer_call(
            h, norm_mix_pre[l][None, :], w_in[l].astype(_BF16),
            jnp.repeat(attn_sinks[l].reshape(N_KV_HEADS, Q_PER_KV), BLOCK, axis=1),
            gmlp_ln_g[l][None, :], gmlp_ln_b[l][None, :], gmlp_w_s[l],
            gmlp_b_s[l].T, w_attn_branch[l].astype(_BF16),
            w_gmlp_branch[l].astype(_BF16), w_out[l].astype(_BF16),
            norm_mix_post[l][None, :], (w_ffn_gate[l], w_ffn_up[l], w_ffn_down[l]))
        h = _ffn_call(
            h.reshape(b * s, d), norm_ffn_pre[l][None, :], wg, wu, wd,
            norm_ffn_post[l][None, :]).reshape(b, s, d)
    return h
```

```python
import functools

import jax
import jax.numpy as jnp
import numpy as np
from jax import lax
from jax.experimental import pallas as pl
from jax.experimental.pallas import tpu as pltpu

D_MODEL = 1024
EPS = 1e-6
LN_EPS = 1e-5
N_Q_HEADS = 8
N_KV_HEADS = 2
HEAD_DIM = 64
Q_PER_KV = N_Q_HEADS // N_KV_HEADS
ATTN_WIDTH = N_Q_HEADS * HEAD_DIM
KV_WIDTH = N_KV_HEADS * HEAD_DIM
BLOCK = 128
GMLP_GROUPS = 4
GMLP_WIDTH = 512
D_FF = 2816
Q_OFF = 0
K_OFF = ATTN_WIDTH
V_OFF = K_OFF + KV_WIDTH
Z_OFF = V_OFF + KV_WIDTH
GA_OFF = Z_OFF + 2 * GMLP_WIDTH
GB_OFF = GA_OFF + D_MODEL
IN_WIDTH = GB_OFF + D_MODEL

V7X_LANES = 128
BF16_SUBLANES = 16
V7X_VMEM_BYTES = 64 * 1024 * 1024
V7X_VMEM_RESERVE_BYTES = 8 * 1024 * 1024
MIXER_TILE = 1024
MIXER_SUB = 256
CAST_PHASE = 1
FFN_TILE = 1024
FFN_SUB = 256

_BF16 = jnp.bfloat16
_F32 = jnp.float32
_SQRT_HALF = np.float32(np.sqrt(0.5))
_MASKED = -1e30


def _dot(a, b):
    return jnp.dot(a, b, preferred_element_type=_F32)


def _rms_norm(x, gain):
    y = x * lax.rsqrt(jnp.mean(x * x, axis=-1, keepdims=True) + EPS)
    return y * gain


def _gelu(x):
    return 0.5 * x * (1.0 + lax.erf(x * _SQRT_HALF))


def _sigmoid(x):
    return 0.5 * jnp.tanh(0.5 * x) + 0.5


def _emit_round_robin(phases_per_sub):
    for phase_of_each in zip(*phases_per_sub):
        for phase in phase_of_each:
            for item in phase:
                item()


def _attention_consts(sinks_ref):
    ki = lax.broadcasted_iota(jnp.int32, (2 * BLOCK, BLOCK), 0)
    qi = lax.broadcasted_iota(jnp.int32, (2 * BLOCK, BLOCK), 1)
    rel = BLOCK + qi - ki
    band = (rel >= 0) & (rel < BLOCK)
    first = band & ((ki >= BLOCK) | (pl.program_id(1) > 0))
    relf = rel.astype(_F32)
    consts = {"valid": jnp.concatenate([band] * Q_PER_KV, axis=1),
              "valid_first": jnp.concatenate([first] * Q_PER_KV, axis=1)}
    for h in range(N_KV_HEADS):
        heads = range(h * Q_PER_KV, (h + 1) * Q_PER_KV)
        consts["alibi", h] = jnp.concatenate(
            [relf * -(2.0 ** (-8.0 * (hq + 1) / N_Q_HEADS)) for hq in heads], axis=1)
        consts["sink", h] = sinks_ref[h:h + 1, :]
    return consts


def _mixer_items(t, r0, sub, tile, states, consts, x_ref, gpre_ref, win_ref, sinks_ref,
                 lng_ref, lnb_ref, ws_ref, bst_ref, wa_ref, wb_ref, wo_ref, gpost_ref,
                 h_ref, kt_ref, vT_ref):
    nb = sub // BLOCK
    st = states[t]
    chains = [(i, h) for i in range(nb) for h in range(N_KV_HEADS)]
    by_name = {}

    def add(name):
        def register(fn):
            by_name[name] = fn
            return fn
        return register

    @add("xn")
    def _():
        st["x"] = x_ref[pl.ds(r0, sub), :]
        st["xn"] = _rms_norm(st["x"], gpre_ref[...]).astype(_BF16)

    @add("qkv")
    def _():
        st["qkv"] = _dot(st["xn"], win_ref[:, Q_OFF:Z_OFF])

    @add("prep")
    def _():
        qkv = st["qkv"]
        q = (qkv[:, Q_OFF:K_OFF] * (HEAD_DIM ** -0.5)).astype(_BF16)
        lo = lax.broadcasted_iota(jnp.int32, (sub, V7X_LANES), 1) < HEAD_DIM
        kf = qkv[:, K_OFF:V_OFF]
        kr = pltpu.roll(kf, HEAD_DIM, 1)
        kt = jnp.concatenate(
            [jnp.where(lo, kf, kr), jnp.where(lo, kr, kf)], axis=1).astype(_BF16)
        vT = qkv[:, V_OFF:Z_OFF].T.astype(_BF16)
        slot = lax.rem(pl.program_id(1), 2)
        if t == 0:
            kt_prev, vT_prev = kt_ref[slot], vT_ref[slot]
        else:
            kt_prev, vT_prev = states[t - 1]["kt_last"], states[t - 1]["vT_last"]
        st["kt_last"], st["vT_last"] = kt[sub - BLOCK:, :], vT[:, sub - BLOCK:]
        if r0 + sub == tile:
            kt_ref[1 - slot] = st["kt_last"]
            vT_ref[1 - slot] = st["vT_last"]
        lo_b = lax.broadcasted_iota(jnp.int32, (BLOCK, V7X_LANES), 1) < HEAD_DIM
        for i, h in chains:
            rows = slice(i * BLOCK, (i + 1) * BLOCK)
            prev = slice((i - 1) * BLOCK, i * BLOCK)
            cols = slice(h * 2 * HEAD_DIM, (h + 1) * 2 * HEAD_DIM)
            st["kt2", i, h] = jnp.concatenate(
                [kt_prev[:, cols] if i == 0 else kt[prev, cols], kt[rows, cols]], axis=0)
            hd = slice(h * HEAD_DIM, (h + 1) * HEAD_DIM)
            st["vT2", i, h] = jnp.concatenate(
                [vT_prev[hd, :] if i == 0 else vT[hd, prev], vT[hd, rows]], axis=1)
            qh = q[rows, h * Q_PER_KV * HEAD_DIM:(h + 1) * Q_PER_KV * HEAD_DIM]
            q_lo, q_hi = qh[:, :V7X_LANES], qh[:, V7X_LANES:]
            zq = jnp.zeros_like(q_lo)
            st["q4", i, h] = jnp.concatenate(
                [jnp.where(lo_b, q_lo, zq), jnp.where(lo_b, zq, q_lo),
                 jnp.where(lo_b, q_hi, zq), jnp.where(lo_b, zq, q_hi)], axis=0)

    @add("zg")
    def _():
        st["zg"] = _dot(st["xn"], win_ref[:, Z_OFF:GA_OFF])

    @add("gelu_u")
    def _():
        st["u"] = _gelu(st["zg"][:, :GMLP_WIDTH])

    @add("gelu_v")
    def _():
        v = _gelu(st["zg"][:, GMLP_WIDTH:])
        mu = jnp.mean(v, axis=-1, keepdims=True)
        vc = v - mu
        var = jnp.mean(vc * vc, axis=-1, keepdims=True)
        st["vn"] = (vc * lax.rsqrt(var + LN_EPS) * lng_ref[...] + lnb_ref[...]).astype(_BF16)

    for c, (i, h) in enumerate(chains):
        @add(f"qk{c}")
        def _(i=i, h=h):
            st["s", i, h] = lax.dot_general(
                st["kt2", i, h], st["q4", i, h], (((1,), (1,)), ((), ())),
                preferred_element_type=_F32)

    for c, (i, h) in enumerate(chains):
        @add(f"softmax{c}")
        def _(i=i, h=h):
            valid = consts["valid_first" if r0 == 0 and i == 0 else "valid"]
            sink = consts["sink", h]
            logits = jnp.where(valid, st["s", i, h] + consts["alibi", h], _MASKED)
            m = jnp.maximum(jnp.max(logits, axis=0, keepdims=True), sink)
            p = jnp.exp(logits - m)
            denom = jnp.sum(p, axis=0, keepdims=True) + jnp.exp(sink - m)
            st["p", i, h] = p.astype(_BF16)
            st["rdenom", i, h] = 1.0 / denom

    @add("ga")
    def _():
        st["ga"] = _dot(st["xn"], win_ref[:, GA_OFF:GB_OFF])

    @add("sig_a")
    def _():
        st["ga"] = _sigmoid(st["ga"])

    @add("gb")
    def _():
        st["gb"] = _dot(st["xn"], win_ref[:, GB_OFF:IN_WIDTH])

    @add("sig_b")
    def _():
        st["gb"] = _sigmoid(st["gb"])

    @add("spatial")
    def _():
        ti = lax.broadcasted_iota(jnp.int32, (BLOCK, BLOCK), 0)
        si = lax.broadcasted_iota(jnp.int32, (BLOCK, BLOCK), 1)
        causal = si <= ti
        vn = st["vn"]
        f_cols = []
        for g in range(GMLP_GROUPS):
            cols = slice(g * BLOCK, (g + 1) * BLOCK)
            wg = jnp.where(causal, ws_ref[g], 0.0).astype(_BF16)
            rhs = jnp.concatenate(
                [vn[i * BLOCK:(i + 1) * BLOCK, cols] for i in range(nb)], axis=1)
            f_cols.append(_dot(wg, rhs) + bst_ref[:, g:g + 1])
        st["f_cols"] = f_cols

    @add("gm")
    def _():
        f = jnp.concatenate(
            [jnp.concatenate([fc[:, i * BLOCK:(i + 1) * BLOCK] for fc in st["f_cols"]], axis=1)
             for i in range(nb)], axis=0)
        st["gm"] = (st["u"] * f).astype(_BF16)

    for c, (i, h) in enumerate(chains):
        @add(f"pv{c}")
        def _(i=i, h=h):
            st["o", i, h] = (_dot(st["vT2", i, h], st["p", i, h])
                             * st["rdenom", i, h])

    @add("br_b")
    def _():
        st["br_b"] = _dot(st["gm"], wb_ref[...])

    @add("br_a")
    def _():
        attn_t = jnp.concatenate(
            [jnp.concatenate(
                [st["o", i, h][:, g * BLOCK:(g + 1) * BLOCK]
                 for h in range(N_KV_HEADS) for g in range(Q_PER_KV)], axis=0)
             for i in range(nb)], axis=1).astype(_BF16)
        st["br_a"] = lax.dot_general(attn_t, wa_ref[...], (((0,), (0,)), ((), ())),
                                     preferred_element_type=_F32)

    @add("merged")
    def _():
        st["merged"] = (st["ga"] * st["br_a"] + st["gb"] * st["br_b"]).astype(_BF16)

    @add("mix")
    def _():
        st["mix"] = _dot(st["merged"], wo_ref[...])

    @add("final")
    def _():
        h_ref[pl.ds(r0, sub), :] = st["x"] + _rms_norm(st["mix"], gpost_ref[...])

    n = range(len(chains))
    phases = [["xn"], ["qkv"], ["zg", "prep"], [f"qk{c}" for c in n],
              ["gelu_u", "gelu_v", "ga", "sig_a"], [f"softmax{c}" for c in n] + ["gb", "sig_b"],
              ["spatial"], [f"pv{c}" for c in n], ["gm", "br_b"], ["br_a"],
              ["merged", "mix"], ["final"]]
    assert sorted(sum(phases, [])) == sorted(by_name)
    return [[by_name[name] for name in phase] for phase in phases]


def _mixer_kernel(*refs, tile, sub, n_cast):
    n_in = len(refs) - 2 * n_cast - 3
    mixer_refs = refs[:n_in] + (refs[n_in + n_cast],) + refs[-2:]
    kt_ref, vT_ref = refs[-2:]

    @pl.when(pl.program_id(1) == 0)
    def _():
        kt_ref[...] = jnp.zeros_like(kt_ref)
        vT_ref[...] = jnp.zeros_like(vT_ref)

    n_sub = tile // sub
    states = [dict() for _ in range(n_sub)]
    consts = _attention_consts(mixer_refs[3])
    phases_per_sub = [_mixer_items(t, t * sub, sub, tile, states, consts, *mixer_refs)
                      for t in range(n_sub)]

    def cast_weights():
        for src, dst in zip(refs[n_in:n_in + n_cast], refs[n_in + n_cast + 1:-2]):
            dst[...] = src[...].astype(_BF16)
    phases_per_sub[-1][CAST_PHASE].append(cast_weights)
    _emit_round_robin(phases_per_sub)


def _ffn_phases(r0, sub, h_ref, gpre_ref, wg_ref, wu_ref, wd_ref, gpost_ref, o_ref):
    st = {}
    rows = pl.ds(r0, sub)

    def norm():
        st["h"] = h_ref[rows, :]
        st["hn"] = _rms_norm(st["h"], gpre_ref[...]).astype(_BF16)

    def gate():
        st["gate"] = _dot(st["hn"], wg_ref[...])

    def up():
        st["up"] = _dot(st["hn"], wu_ref[...])

    def down():
        act = (st["gate"] * _sigmoid(st["gate"]) * st["up"]).astype(_BF16)
        st["ff"] = _dot(act, wd_ref[...])

    def final():
        o_ref[rows, :] = st["h"] + _rms_norm(st["ff"], gpost_ref[...])

    return [[norm], [gate], [up], [down], [final]]


def _ffn_kernel(*refs, tile, sub):
    _emit_round_robin([_ffn_phases(r0, sub, *refs) for r0 in range(0, tile, sub)])


def _resident(shape):
    zeros = (0,) * len(shape)
    return pl.BlockSpec(shape, lambda *_: zeros, pipeline_mode=pl.Buffered(1))


def _vmem_limit(resident_bytes, block_bytes):
    budget = V7X_VMEM_BYTES - V7X_VMEM_RESERVE_BYTES
    assert resident_bytes + 6 * block_bytes <= budget, (resident_bytes, block_bytes)
    return budget


def _cast_spec(shape, n_steps, steps_per_seq):
    rows, cols = shape
    for n_col in range(1, n_steps + 1):
        n_row = n_steps // n_col
        if (n_row * n_col == n_steps and rows % n_row == 0 and cols % n_col == 0
                and (rows // n_row) % BF16_SUBLANES == 0 and (cols // n_col) % V7X_LANES == 0):
            def index_map(i, j, n_col=n_col):
                step = i * steps_per_seq + j
                return step // n_col, step % n_col
            return pl.BlockSpec((rows // n_row, cols // n_col), index_map)
    raise ValueError(f"no aligned {n_steps}-block split of {shape}")


def _mixer_call(x, gpre, win, sinks, lng, lnb, ws, bst, wa, wb, wo, gpost, cast_ws):
    b, s, d = x.shape
    tile, sub = MIXER_TILE, MIXER_SUB
    assert s % tile == 0 and tile % sub == 0 and sub % BLOCK == 0
    steps_per_seq = s // tile
    cast_specs = [_cast_spec(w.shape, b * steps_per_seq, steps_per_seq) for w in cast_ws]
    weights_bytes = 2 * (win.size + wa.size + wb.size + wo.size) + 4 * ws.size
    tok =pl.BlockSpec((None, tile, d), lambda i, j: (i, j, 0))
    return pl.pallas_call(
        functools.partial(_mixer_kernel, tile=tile, sub=sub, n_cast=len(cast_ws)),
        grid=(b, s // tile),
        in_specs=[
            tok,
            _resident(gpre.shape),
            _resident(win.shape),
            _resident(sinks.shape),
            _resident(lng.shape),
            _resident(lnb.shape),
            _resident(ws.shape),
            _resident(bst.shape),
            _resident(wa.shape),
            _resident(wb.shape),
            _resident(wo.shape),
            _resident(gpost.shape),
        ] + cast_specs,
        out_specs=[tok] + cast_specs,
        out_shape=[jax.ShapeDtypeStruct(x.shape, x.dtype)]
        + [jax.ShapeDtypeStruct(w.shape, _BF16) for w in cast_ws],
        scratch_shapes=[pltpu.VMEM((2, BLOCK, 2 * V7X_LANES), _BF16),
                        pltpu.VMEM((2, KV_WIDTH, BLOCK), _BF16)],
        compiler_params=pltpu.CompilerParams(
            dimension_semantics=("arbitrary", "arbitrary"),
            vmem_limit_bytes=_vmem_limit(weights_bytes, tile * d * x.dtype.itemsize)),
        name="mixer",
    )(x, gpre, win, sinks, lng, lnb, ws, bst, wa, wb, wo, gpost, *cast_ws)


def _ffn_call(h2d, gpre, wg, wu, wd, gpost):
    m, d = h2d.shape
    tile, sub = FFN_TILE, FFN_SUB
    assert m % tile == 0 and tile % sub == 0
    weights_bytes = 2 * (wg.size + wu.size + wd.size)
    row =pl.BlockSpec((tile, d), lambda i: (i, 0))
    return pl.pallas_call(
        functools.partial(_ffn_kernel, tile=tile, sub=sub),
        grid=(m // tile,),
        in_specs=[row, _resident(gpre.shape), _resident(wg.shape), _resident(wu.shape),
                  _resident(wd.shape), _resident(gpost.shape)],
        out_specs=row,
        out_shape=jax.ShapeDtypeStruct(h2d.shape, h2d.dtype),
        compiler_params=pltpu.CompilerParams(
            dimension_semantics=("arbitrary",),
            vmem_limit_bytes=_vmem_limit(weights_bytes, tile * d * h2d.dtype.itemsize)),
        name="ffn",
    )(h2d, gpre, wg, wu, wd, gpost)


def kernel(x, norm_mix_pre, w_in, attn_sinks, gmlp_ln_g, gmlp_ln_b, gmlp_w_s, gmlp_b_s,
           w_attn_branch, w_gmlp_branch, w_out, norm_mix_post, norm_ffn_pre,
           w_ffn_gate, w_ffn_up, w_ffn_down, norm_ffn_post):
    b, s, d = x.shape
    depth = w_in.shape[0]
    h = x
    for l in range(depth):
        h, wg, wu, wd = _mixer_call(
            h, norm_mix_pre[l][None, :], w_in[l].astype(_BF16),
            jnp.repeat(attn_sinks[l].reshape(N_KV_HEADS, Q_PER_KV), BLOCK, axis=1),
            gmlp_ln_g[l][None, :], gmlp_ln_b[l][None, :], gmlp_w_s[l],
            gmlp_b_s[l].T, w_attn_branch[l].astype(_BF16),
            w_gmlp_branch[l].astype(_BF16), w_out[l].astype(_BF16),
            norm_mix_post[l][None, :], (w_ffn_gate[l], w_ffn_up[l], w_ffn_down[l]))
        h = _ffn_call(
            h.reshape(b * s, d), norm_ffn_pre[l][None, :], wg, wu, wd,
            norm_ffn_post[l][None, :]).reshape(b, s, d)
    return h
```

```python
import functools

import jax
import jax.numpy as jnp
import numpy as np
from jax import lax
from jax.experimental import pallas as pl
from jax.experimental.pallas import tpu as pltpu

D_MODEL = 1024
EPS = 1e-6
LN_EPS = 1e-5
N_Q_HEADS = 8
N_KV_HEADS = 2
HEAD_DIM = 64
Q_PER_KV = N_Q_HEADS // N_KV_HEADS
ATTN_WIDTH = N_Q_HEADS * HEAD_DIM
KV_WIDTH = N_KV_HEADS * HEAD_DIM
BLOCK = 128
GMLP_GROUPS = 4
GMLP_WIDTH = 512
D_FF = 2816
Q_OFF = 0
K_OFF = ATTN_WIDTH
V_OFF = K_OFF + KV_WIDTH
Z_OFF = V_OFF + KV_WIDTH
GA_OFF = Z_OFF + 2 * GMLP_WIDTH
GB_OFF = GA_OFF + D_MODEL
IN_WIDTH = GB_OFF + D_MODEL

V7X_LANES = 128
BF16_SUBLANES = 16
V7X_VMEM_BYTES = 64 * 1024 * 1024
V7X_VMEM_RESERVE_BYTES = 8 * 1024 * 1024
MIXER_TILE = 1024
MIXER_SUB = 256
CAST_PHASE = 1
FFN_TILE = 1024
FFN_SUB = 256
FFN_CHUNK = 1536

_BF16 = jnp.bfloat16
_F32 = jnp.float32
_SQRT_HALF = np.float32(np.sqrt(0.5))
_MASKED = -1e30


def _dot(a, b):
    return jnp.dot(a, b, preferred_element_type=_F32)


def _rms_norm(x, gain):
    y = x * lax.rsqrt(jnp.mean(x * x, axis=-1, keepdims=True) + EPS)
    return y * gain


def _gelu(x):
    return 0.5 * x * (1.0 + lax.erf(x * _SQRT_HALF))


def _sigmoid(x):
    return 0.5 * jnp.tanh(0.5 * x) + 0.5


def _emit_round_robin(phases_per_sub):
    for phase_of_each in zip(*phases_per_sub):
        for phase in phase_of_each:
            for item in phase:
                item()


def _attention_consts(sinks_ref):
    ki = lax.broadcasted_iota(jnp.int32, (2 * BLOCK, BLOCK), 0)
    qi = lax.broadcasted_iota(jnp.int32, (2 * BLOCK, BLOCK), 1)
    rel = BLOCK + qi - ki
    band = (rel >= 0) & (rel < BLOCK)
    first = band & ((ki >= BLOCK) | (pl.program_id(1) > 0))
    relf = rel.astype(_F32)
    consts = {"valid": jnp.concatenate([band] * Q_PER_KV, axis=1),
              "valid_first": jnp.concatenate([first] * Q_PER_KV, axis=1)}
    for h in range(N_KV_HEADS):
        heads = range(h * Q_PER_KV, (h + 1) * Q_PER_KV)
        consts["alibi", h] = jnp.concatenate(
            [relf * -(2.0 ** (-8.0 * (hq + 1) / N_Q_HEADS)) for hq in heads], axis=1)
        consts["sink", h] = sinks_ref[h:h + 1, :]
    return consts


def _mixer_items(t, r0, sub, tile, states, consts, x_ref, gpre_ref, win_ref, sinks_ref,
                 lng_ref, lnb_ref, ws_ref, bst_ref, wa_ref, wb_ref, wo_ref, gpost_ref,
                 h_ref, kt_ref, vT_ref):
    nb = sub // BLOCK
    st = states[t]
    chains = [(i, h) for i in range(nb) for h in range(N_KV_HEADS)]
    by_name = {}

    def add(name):
        def register(fn):
            by_name[name] = fn
            return fn
        return register

    @add("xn")
    def _():
        st["x"] = x_ref[pl.ds(r0, sub), :]
        st["xn"] = _rms_norm(st["x"], gpre_ref[...]).astype(_BF16)

    @add("qkv")
    def _():
        st["qkv"] = _dot(st["xn"], win_ref[:, Q_OFF:Z_OFF])

    @add("prep")
    def _():
        qkv = st["qkv"]
        q = (qkv[:, Q_OFF:K_OFF] * (HEAD_DIM ** -0.5)).astype(_BF16)
        lo = lax.broadcasted_iota(jnp.int32, (sub, V7X_LANES), 1) < HEAD_DIM
        kf = qkv[:, K_OFF:V_OFF]
        kr = pltpu.roll(kf, HEAD_DIM, 1)
        kt = jnp.concatenate(
            [jnp.where(lo, kf, kr), jnp.where(lo, kr, kf)], axis=1).astype(_BF16)
        vT = qkv[:, V_OFF:Z_OFF].T.astype(_BF16)
        slot = lax.rem(pl.program_id(1), 2)
        if t == 0:
            kt_prev, vT_prev = kt_ref[slot], vT_ref[slot]
        else:
            kt_prev, vT_prev = states[t - 1]["kt_last"], states[t - 1]["vT_last"]
        st["kt_last"], st["vT_last"] = kt[sub - BLOCK:, :], vT[:, sub - BLOCK:]
        if r0 + sub == tile:
            kt_ref[1 - slot] = st["kt_last"]
            vT_ref[1 - slot] = st["vT_last"]
        lo_b = lax.broadcasted_iota(jnp.int32, (BLOCK, V7X_LANES), 1) < HEAD_DIM
        for i, h in chains:
            rows = slice(i * BLOCK, (i + 1) * BLOCK)
            prev = slice((i - 1) * BLOCK, i * BLOCK)
            cols = slice(h * 2 * HEAD_DIM, (h + 1) * 2 * HEAD_DIM)
            st["kt2", i, h] = jnp.concatenate(
                [kt_prev[:, cols] if i == 0 else kt[prev, cols], kt[rows, cols]], axis=0)
            hd = slice(h * HEAD_DIM, (h + 1) * HEAD_DIM)
            st["vT2", i, h] = jnp.concatenate(
                [vT_prev[hd, :] if i == 0 else vT[hd, prev], vT[hd, rows]], axis=1)
            qh = q[rows, h * Q_PER_KV * HEAD_DIM:(h + 1) * Q_PER_KV * HEAD_DIM]
            q_lo, q_hi = qh[:, :V7X_LANES], qh[:, V7X_LANES:]
            zq = jnp.zeros_like(q_lo)
            st["q4", i, h] = jnp.concatenate(
                [jnp.where(lo_b, q_lo, zq), jnp.where(lo_b, zq, q_lo),
                 jnp.where(lo_b, q_hi, zq), jnp.where(lo_b, zq, q_hi)], axis=0)

    @add("zg")
    def _():
        st["zg"] = _dot(st["xn"], win_ref[:, Z_OFF:GA_OFF])

    @add("gelu_u")
    def _():
        st["u"] = _gelu(st["zg"][:, :GMLP_WIDTH])

    @add("gelu_v")
    def _():
        v = _gelu(st["zg"][:, GMLP_WIDTH:])
        mu = jnp.mean(v, axis=-1, keepdims=True)
        vc = v - mu
        var = jnp.mean(vc * vc, axis=-1, keepdims=True)
        st["vn"] = (vc * lax.rsqrt(var + LN_EPS) * lng_ref[...] + lnb_ref[...]).astype(_BF16)

    for c, (i, h) in enumerate(chains):
        @add(f"qk{c}")
        def _(i=i, h=h):
            st["s", i, h] = lax.dot_general(
                st["kt2", i, h], st["q4", i, h], (((1,), (1,)), ((), ())),
                preferred_element_type=_F32)

    for c, (i, h) in enumerate(chains):
        @add(f"softmax{c}")
        def _(i=i, h=h):
            valid = consts["valid_first" if r0 == 0 and i == 0 else "valid"]
            sink = consts["sink", h]
            logits = jnp.where(valid, st["s", i, h] + consts["alibi", h], _MASKED)
            m = jnp.maximum(jnp.max(logits, axis=0, keepdims=True), sink)
            p = jnp.exp(logits - m)
            denom = jnp.sum(p, axis=0, keepdims=True) + jnp.exp(sink - m)
            st["p", i, h] = p.astype(_BF16)
            st["rdenom", i, h] = 1.0 / denom

    @add("ga")
    def _():
        st["ga"] = _dot(st["xn"], win_ref[:, GA_OFF:GB_OFF])

    @add("sig_a")
    def _():
        st["ga"] = _sigmoid(st["ga"])

    @add("gb")
    def _():
        st["gb"] = _dot(st["xn"], win_ref[:, GB_OFF:IN_WIDTH])

    @add("sig_b")
    def _():
        st["gb"] = _sigmoid(st["gb"])

    @add("spatial")
    def _():
        ti = lax.broadcasted_iota(jnp.int32, (BLOCK, BLOCK), 0)
        si = lax.broadcasted_iota(jnp.int32, (BLOCK, BLOCK), 1)
        causal = si <= ti
        vn = st["vn"]
        f_cols = []
        for g in range(GMLP_GROUPS):
            cols = slice(g * BLOCK, (g + 1) * BLOCK)
            wg = jnp.where(causal, ws_ref[g], 0.0).astype(_BF16)
            rhs = jnp.concatenate(
                [vn[i * BLOCK:(i + 1) * BLOCK, cols] for i in range(nb)], axis=1)
            f_cols.append(_dot(wg, rhs) + bst_ref[:, g:g + 1])
        st["f_cols"] = f_cols

    @add("gm")
    def _():
        f = jnp.concatenate(
            [jnp.concatenate([fc[:, i * BLOCK:(i + 1) * BLOCK] for fc in st["f_cols"]], axis=1)
             for i in range(nb)], axis=0)
        st["gm"] = (st["u"] * f).astype(_BF16)

    for c, (i, h) in enumerate(chains):
        @add(f"pv{c}")
        def _(i=i, h=h):
            st["o", i, h] = (_dot(st["vT2", i, h], st["p", i, h])
                             * st["rdenom", i, h])

    @add("br_b")
    def _():
        st["br_b"] = _dot(st["gm"], wb_ref[...])

    @add("br_a")
    def _():
        attn_t = jnp.concatenate(
            [jnp.concatenate(
                [st["o", i, h][:, g * BLOCK:(g + 1) * BLOCK]
                 for h in range(N_KV_HEADS) for g in range(Q_PER_KV)], axis=0)
             for i in range(nb)], axis=1).astype(_BF16)
        st["br_a"] = lax.dot_general(attn_t, wa_ref[...], (((0,), (0,)), ((), ())),
                                     preferred_element_type=_F32)

    @add("merged")
    def _():
        st["merged"] = (st["ga"] * st["br_a"] + st["gb"] * st["br_b"]).astype(_BF16)

    @add("mix")
    def _():
        st["mix"] = _dot(st["merged"], wo_ref[...])

    @add("final")
    def _():
        h_ref[pl.ds(r0, sub), :] = st["x"] + _rms_norm(st["mix"], gpost_ref[...])

    n = range(len(chains))
    phases = [["xn"], ["qkv"], ["zg", "prep"], [f"qk{c}" for c in n],
              ["gelu_u", "gelu_v", "ga", "sig_a"], [f"softmax{c}" for c in n] + ["gb", "sig_b"],
              ["spatial"], [f"pv{c}" for c in n], ["gm", "br_b"], ["br_a"],
              ["merged", "mix"], ["final"]]
    assert sorted(sum(phases, [])) == sorted(by_name)
    return [[by_name[name] for name in phase] for phase in phases]


def _mixer_kernel(*refs, tile, sub, n_cast):
    n_in = len(refs) - 2 * n_cast - 3
    mixer_refs = refs[:n_in] + (refs[n_in + n_cast],) + refs[-2:]
    kt_ref, vT_ref = refs[-2:]

    @pl.when(pl.program_id(1) == 0)
    def _():
        kt_ref[...] = jnp.zeros_like(kt_ref)
        vT_ref[...] = jnp.zeros_like(vT_ref)

    n_sub = tile // sub
    states = [dict() for _ in range(n_sub)]
    consts = _attention_consts(mixer_refs[3])
    phases_per_sub = [_mixer_items(t, t * sub, sub, tile, states, consts, *mixer_refs)
                      for t in range(n_sub)]

    def cast_weights():
        for src, dst in zip(refs[n_in:n_in + n_cast], refs[n_in + n_cast + 1:-2]):
            dst[...] = src[...].astype(_BF16)
    phases_per_sub[-1][CAST_PHASE].append(cast_weights)
    _emit_round_robin(phases_per_sub)


def _ffn_phases(r0, sub, h_ref, gpre_ref, wg_ref, wu_ref, wd_ref, gpost_ref, o_ref):
    st = {}
    rows = pl.ds(r0, sub)

    def norm():
        st["h"] = h_ref[rows, :]
        st["hn"] = _rms_norm(st["h"], gpre_ref[...]).astype(_BF16)

    def gate_up(cols):
        def run():
            st["gate"] = _dot(st["hn"], wg_ref[:, cols])
            st["up"] = _dot(st["hn"], wu_ref[:, cols])
        return run

    def down(cols, first):
        def run():
            act = (st["gate"] * _sigmoid(st["gate"]) * st["up"]).astype(_BF16)
            part = _dot(act, wd_ref[cols, :])
            st["ff"] = part if first else st["ff"] + part
        return run

    def final():
        o_ref[rows, :] = st["h"] + _rms_norm(st["ff"], gpost_ref[...])

    phases = [[norm]]
    for c, lo in enumerate(range(0, D_FF, FFN_CHUNK)):
        cols = slice(lo, min(lo + FFN_CHUNK, D_FF))
        phases += [[gate_up(cols)], [down(cols, c == 0)]]
    return phases + [[final]]


def _ffn_kernel(*refs, tile, sub):
    _emit_round_robin([_ffn_phases(r0, sub, *refs) for r0 in range(0, tile, sub)])


def _resident(shape):
    zeros = (0,) * len(shape)
    return pl.BlockSpec(shape, lambda *_: zeros, pipeline_mode=pl.Buffered(1))


def _vmem_limit(resident_bytes, block_bytes):
    budget = V7X_VMEM_BYTES - V7X_VMEM_RESERVE_BYTES
    assert resident_bytes + 6 * block_bytes <= budget, (resident_bytes, block_bytes)
    return budget


def _cast_spec(shape, n_steps, steps_per_seq):
    rows, cols = shape
    for n_col in range(1, n_steps + 1):
        n_row = n_steps // n_col
        if (n_row * n_col == n_steps and rows % n_row == 0 and cols % n_col == 0
                and (rows // n_row) % BF16_SUBLANES == 0 and (cols // n_col) % V7X_LANES == 0):
            def index_map(i, j, n_col=n_col):
                step = i * steps_per_seq + j
                return step // n_col, step % n_col
            return pl.BlockSpec((rows // n_row, cols // n_col), index_map)
    raise ValueError(f"no aligned {n_steps}-block split of {shape}")


def _mixer_call(x, gpre, win, sinks, lng, lnb, ws, bst, wa, wb, wo, gpost, cast_ws):
    b, s, d = x.shape
    tile, sub = MIXER_TILE, MIXER_SUB
    assert s % tile == 0 and tile % sub == 0 and sub % BLOCK == 0
    steps_per_seq = s // tile
    cast_specs = [_cast_spec(w.shape, b * steps_per_seq, steps_per_seq) for w in cast_ws]
    weights_bytes = 2 * (win.size + wa.size + wb.size + wo.size) + 4 * ws.size
    tok =pl.BlockSpec((None, tile, d), lambda i, j: (i, j, 0))
    return pl.pallas_call(
        functools.partial(_mixer_kernel, tile=tile, sub=sub, n_cast=len(cast_ws)),
        grid=(b, s // tile),
        in_specs=[
            tok,
            _resident(gpre.shape),
            _resident(win.shape),
            _resident(sinks.shape),
            _resident(lng.shape),
            _resident(lnb.shape),
            _resident(ws.shape),
            _resident(bst.shape),
            _resident(wa.shape),
            _resident(wb.shape),
            _resident(wo.shape),
            _resident(gpost.shape),
        ] + cast_specs,
        out_specs=[tok] + cast_specs,
        out_shape=[jax.ShapeDtypeStruct(x.shape, x.dtype)]
        + [jax.ShapeDtypeStruct(w.shape, _BF16) for w in cast_ws],
        scratch_shapes=[pltpu.VMEM((2, BLOCK, 2 * V7X_LANES), _BF16),
                        pltpu.VMEM((2, KV_WIDTH, BLOCK), _BF16)],
        compiler_params=pltpu.CompilerParams(
            dimension_semantics=("arbitrary", "arbitrary"),
            vmem_limit_bytes=_vmem_limit(weights_bytes, tile * d * x.dtype.itemsize)),
        name="mixer",
    )(x, gpre, win, sinks, lng, lnb, ws, bst, wa, wb, wo, gpost, *cast_ws)


def _ffn_call(h2d, gpre, wg, wu, wd, gpost):
    m, d = h2d.shape
    tile, sub = FFN_TILE, FFN_SUB
    assert m % tile == 0 and tile % sub == 0
    weights_bytes = 2 * (wg.size + wu.size + wd.size)
    row =pl.BlockSpec((tile, d), lambda i: (i, 0))
    return pl.pallas_call(
        functools.partial(_ffn_kernel, tile=tile, sub=sub),
        grid=(m // tile,),
        in_specs=[row, _resident(gpre.shape), _resident(wg.shape), _resident(wu.shape),
                  _resident(wd.shape), _resident(gpost.shape)],
        out_specs=row,
        out_shape=jax.ShapeDtypeStruct(h2d.shape, h2d.dtype),
        compiler_params=pltpu.CompilerParams(
            dimension_semantics=("arbitrary",),
            vmem_limit_bytes=_vmem_limit(weights_bytes, tile * d * h2d.dtype.itemsize)),
        name="ffn",
    )(h2d, gpre, wg, wu, wd, gpost)


def kernel(x, norm_mix_pre, w_in, attn_sinks, gmlp_ln_g, gmlp_ln_b, gmlp_w_s, gmlp_b_s,
           w_attn_branch, w_gmlp_branch, w_out, norm_mix_post, norm_ffn_pre,
           w_ffn_gate, w_ffn_up, w_ffn_down, norm_ffn_post):
    b, s, d = x.shape
    depth = w_in.shape[0]
    h = x
    for l in range(depth):
        h, wg, wu, wd = _mixer_call(
            h, norm_mix_pre[l][None, :], w_in[l].astype(_BF16),
            jnp.repeat(attn_sinks[l].reshape(N_KV_HEADS, Q_PER_KV), BLOCK, axis=1),
            gmlp_ln_g[l][None, :], gmlp_ln_b[l][None, :], gmlp_w_s[l],
            gmlp_b_s[l].T, w_attn_branch[l].astype(_BF16),
            w_gmlp_branch[l].astype(_BF16), w_out[l].astype(_BF16),
            norm_mix_post[l][None, :], (w_ffn_gate[l], w_ffn_up[l], w_ffn_down[l]))
        h = _ffn_call(
            h.reshape(b * s, d), norm_ffn_pre[l][None, :], wg, wu, wd,
            norm_ffn_post[l][None, :]).reshape(b, s, d)
    return h
```
